```python
import math
import jax, jax.numpy as jnp
from jax import lax
import numpy as np

D_MODEL = 1024
BATCH = 16
SEQ = 2048
DEPTH = 4

N_MIXERS = 3
N_HEADS = 8
HEAD_DIM = 128
WIDTH = N_HEADS * HEAD_DIM
ROPE_THETA = 10000.0
NORM_EPS = 1e-6
Q_BLOCK = 128
MOBA_BLOCK = 256
MOBA_TOPK = 3
MOBA_Q_CHUNK = 8
DIFF_SUB_DIM = HEAD_DIM // 2
DIFF_LAMBDA_STD = 0.1

kernel_name = "hybrid_moba_fox_diffattn_gated"


def _layers_of(kind):
    return len(range(kind, DEPTH, N_MIXERS))


def rms_norm(x, g):
    x32 = x.astype(jnp.float32)
    y = x32 * lax.rsqrt(jnp.mean(x32 * x32, axis=-1, keepdims=True) + NORM_EPS)
    return (y * g.astype(jnp.float32)).astype(x.dtype)


def rope_tables(seq, dim):
    inv = 1.0 / (ROPE_THETA ** (jnp.arange(0, dim, 2, dtype=jnp.float32) / dim))
    ang = jnp.arange(seq, dtype=jnp.float32)[:, None] * inv[None, :]
    return jnp.cos(ang), jnp.sin(ang)


def apply_rope(x, cos, sin):
    x32 = x.astype(jnp.float32)
    x1, x2 = jnp.split(x32, 2, axis=-1)
    out = jnp.concatenate([x1 * cos - x2 * sin, x2 * cos + x1 * sin], axis=-1)
    return out.astype(x.dtype)


def split_heads(t, n):
    b, s, _ = t.shape
    return t.reshape(b, s, n, -1).transpose(0, 2, 1, 3)


def merge_heads(t):
    b, h, s, d = t.shape
    return t.transpose(0, 2, 1, 3).reshape(b, s, h * d)


def gather_blocks(blocks, idx):
    return jax.vmap(jax.vmap(lambda bl, ix: bl[ix]))(blocks, idx)


def moba_attention(q, k, v):
    b, h, s, d = q.shape
    nb = -(-s // MOBA_BLOCK)
    pad = nb * MOBA_BLOCK - s
    kb = jnp.pad(k, ((0, 0), (0, 0), (0, pad), (0, 0))).reshape(b, h, nb, MOBA_BLOCK, d)
    vb = jnp.pad(v, ((0, 0), (0, 0), (0, pad), (0, 0))).reshape(b, h, nb, MOBA_BLOCK, d)
    topk = min(MOBA_TOPK, nb - 1)
    scale = d ** -0.5
    if topk > 0:
        k_mean = jnp.mean(kb.astype(jnp.float32), axis=3)
        gate = jnp.einsum('bhsd,bhnd->bhsn', q, k_mean.astype(q.dtype),
                          preferred_element_type=jnp.float32)
        q_blk = jnp.arange(s) // MOBA_BLOCK
        past = jnp.arange(nb)[None, :] < q_blk[:, None]
        gate = jnp.where(past, gate, -jnp.inf)
        _, sel = lax.top_k(gate, topk)

    def chunk(c):
        t0 = c * MOBA_Q_CHUNK
        qc = lax.dynamic_slice_in_dim(q, t0, MOBA_Q_CHUNK, axis=2)
        tq = t0 + jnp.arange(MOBA_Q_CHUNK)
        own = t0 // MOBA_BLOCK
        k_own = lax.dynamic_index_in_dim(kb, own, axis=2, keepdims=False)
        v_own = lax.dynamic_index_in_dim(vb, own, axis=2, keepdims=False)
        s_own = jnp.einsum('bhqd,bhkd->bhqk', qc, k_own,
                           preferred_element_type=jnp.float32) * scale
        kpos = own * MOBA_BLOCK + jnp.arange(MOBA_BLOCK)
        s_own = jnp.where(kpos[None, :] <= tq[:, None], s_own, -jnp.inf)
        if topk == 0:
            p_own = jax.nn.softmax(s_own, axis=-1)
            return jnp.einsum('bhqk,bhkd->bhqd', p_own.astype(v.dtype), v_own)
        sel_c = lax.dynamic_slice_in_dim(sel, t0, MOBA_Q_CHUNK, axis=2)
        k_sel = gather_blocks(kb, sel_c)
        v_sel = gather_blocks(vb, sel_c)
        s_sel = jnp.einsum('bhqd,bhqnkd->bhqnk', qc, k_sel,
                           preferred_element_type=jnp.float32) * scale
        valid = sel_c < own
        s_sel = jnp.where(valid[..., None], s_sel, -jnp.inf)
        s_sel = s_sel.reshape(b, h, MOBA_Q_CHUNK, topk * MOBA_BLOCK)
        p = jax.nn.softmax(jnp.concatenate([s_sel, s_own], axis=-1), axis=-1).astype(v.dtype)
        p_sel = p[..., :topk * MOBA_BLOCK].reshape(b, h, MOBA_Q_CHUNK, topk, MOBA_BLOCK)
        p_own = p[..., topk * MOBA_BLOCK:]
        return (jnp.einsum('bhqnk,bhqnkd->bhqd', p_sel, v_sel)
                + jnp.einsum('bhqk,bhkd->bhqd', p_own, v_own))

    outs = lax.map(chunk, jnp.arange(s // MOBA_Q_CHUNK))
    return outs.transpose(1, 2, 0, 3, 4).reshape(b, h, s, d)


def forgetting_attention(q, k, v, log_f):
    b, h, s, d = q.shape
    scale = d ** -0.5
    c = jnp.cumsum(log_f, axis=-1)
    kpos = jnp.arange(s)

    def block(i):
        t0 = i * Q_BLOCK
        qb = lax.dynamic_slice_in_dim(q, t0, Q_BLOCK, axis=2)
        cq = lax.dynamic_slice_in_dim(c, t0, Q_BLOCK, axis=2)
        tq = t0 + jnp.arange(Q_BLOCK)
        sc = (jnp.einsum('bhqd,bhkd->bhqk', qb, k, preferred_element_type=jnp.float32) * scale
              + (cq[..., :, None] - c[..., None, :]))
        sc = jnp.where(kpos[None, :] <= tq[:, None], sc, -jnp.inf)
        p = jax.nn.softmax(sc, axis=-1)
        return jnp.einsum('bhqk,bhkd->bhqd', p.astype(v.dtype), v)

    outs = lax.map(block, jnp.arange(s // Q_BLOCK))
    return outs.transpose(1, 2, 0, 3, 4).reshape(b, h, s, d)


def differential_attention(q1, q2, k1, k2, v, lam):
    b, h, s, d = q1.shape
    scale = d ** -0.5
    kpos = jnp.arange(s)

    def block(i):
        t0 = i * Q_BLOCK
        tq = t0 + jnp.arange(Q_BLOCK)
        mask = kpos[None, :] <= tq[:, None]
        qb1 = lax.dynamic_slice_in_dim(q1, t0, Q_BLOCK, axis=2)
        qb2 = lax.dynamic_slice_in_dim(q2, t0, Q_BLOCK, axis=2)
        s1 = jnp.einsum('bhqd,bhkd->bhqk', qb1, k1, preferred_element_type=jnp.float32) * scale
        s2 = jnp.einsum('bhqd,bhkd->bhqk', qb2, k2, preferred_element_type=jnp.float32) * scale
        p = (jax.nn.softmax(jnp.where(mask, s1, -jnp.inf), axis=-1)
             - lam * jax.nn.softmax(jnp.where(mask, s2, -jnp.inf), axis=-1))
        return jnp.einsum('bhqk,bhkd->bhqd', p.astype(v.dtype), v)

    outs = lax.map(block, jnp.arange(s // Q_BLOCK))
    return outs.transpose(1, 2, 0, 3, 4).reshape(b, h, s, v.shape[-1])


def setup_inputs(seed: int = 0) -> dict:
    key = jax.random.key(seed)
    ks = jax.random.split(key, 20)
    na, nbl, nc = _layers_of(0), _layers_of(1), _layers_of(2)
    f32 = jnp.float32
    w_in_scale = D_MODEL ** -0.5
    w_out_scale = WIDTH ** -0.5

    def gain(k, shape):
        return 1.0 + 0.02 * jax.random.normal(k, shape, f32)

    return {
        "x": jax.random.normal(ks[0], (BATCH, SEQ, D_MODEL), f32),
        "norm_g": gain(ks[1], (DEPTH, D_MODEL)),
        "w_out": jax.random.normal(ks[2], (DEPTH, WIDTH, D_MODEL), f32) * w_out_scale,
        "a_w_in": jax.random.normal(ks[3], (na, D_MODEL, 4 * WIDTH), f32) * w_in_scale,
        "a_q_norm": gain(ks[4], (na, HEAD_DIM)),
        "a_k_norm": gain(ks[5], (na, HEAD_DIM)),
        "b_w_in": jax.random.normal(ks[6], (nbl, D_MODEL, 4 * WIDTH + N_HEADS), f32) * w_in_scale,
        "b_f_bias": jax.random.uniform(ks[7], (nbl, N_HEADS), f32, minval=1.0, maxval=4.0),
        "b_q_norm": gain(ks[8], (nbl, HEAD_DIM)),
        "b_k_norm": gain(ks[9], (nbl, HEAD_DIM)),
        "c_w_in": jax.random.normal(ks[10], (nc, D_MODEL, 4 * WIDTH), f32) * w_in_scale,
        "c_q_norm": gain(ks[11], (nc, DIFF_SUB_DIM)),
        "c_k_norm": gain(ks[12], (nc, DIFF_SUB_DIM)),
        "c_lambda_q1": DIFF_LAMBDA_STD * jax.random.normal(ks[13], (nc, DIFF_SUB_DIM), f32),
        "c_lambda_k1": DIFF_LAMBDA_STD * jax.random.normal(ks[14], (nc, DIFF_SUB_DIM), f32),
        "c_lambda_q2": DIFF_LAMBDA_STD * jax.random.normal(ks[15], (nc, DIFF_SUB_DIM), f32),
        "c_lambda_k2": DIFF_LAMBDA_STD * jax.random.normal(ks[16], (nc, DIFF_SUB_DIM), f32),
        "c_subln": gain(ks[17], (nc, HEAD_DIM)),
    }


def reference(x, norm_g, w_out, a_w_in, a_q_norm, a_k_norm, b_w_in, b_f_bias, b_q_norm, b_k_norm,
              c_w_in, c_q_norm, c_k_norm, c_lambda_q1, c_lambda_k1, c_lambda_q2, c_lambda_k2, c_subln):
    b, s, _ = x.shape
    cos_h, sin_h = rope_tables(s, HEAD_DIM)
    cos_d, sin_d = rope_tables(s, DIFF_SUB_DIM)
    for i in range(DEPTH):
        kind = i % N_MIXERS
        j = i // N_MIXERS
        hdn = rms_norm(x, norm_g[i])
        if kind == 0:
            proj = hdn @ a_w_in[j]
            q, k, v, z = jnp.split(proj, 4, axis=-1)
            q = apply_rope(rms_norm(split_heads(q, N_HEADS), a_q_norm[j]), cos_h, sin_h)
            k = apply_rope(rms_norm(split_heads(k, N_HEADS), a_k_norm[j]), cos_h, sin_h)
            y = moba_attention(q, k, split_heads(v, N_HEADS))
        elif kind == 1:
            proj = hdn @ b_w_in[j]
            q, k, v, z = jnp.split(proj[..., :4 * WIDTH], 4, axis=-1)
            f_logit = (proj[..., 4 * WIDTH:] + b_f_bias[j]).astype(jnp.float32)
            log_f = jax.nn.log_sigmoid(f_logit).transpose(0, 2, 1)
            q = rms_norm(split_heads(q, N_HEADS), b_q_norm[j])
            k = rms_norm(split_heads(k, N_HEADS), b_k_norm[j])
            y = forgetting_attention(q, k, split_heads(v, N_HEADS), log_f)
        else:
            proj = hdn @ c_w_in[j]
            q, k, v, z = jnp.split(proj, 4, axis=-1)
            q = q.reshape(b, s, N_HEADS, 2, DIFF_SUB_DIM).transpose(0, 2, 3, 1, 4)
            k = k.reshape(b, s, N_HEADS, 2, DIFF_SUB_DIM).transpose(0, 2, 3, 1, 4)
            q = apply_rope(rms_norm(q, c_q_norm[j]), cos_d, sin_d)
            k = apply_rope(rms_norm(k, c_k_norm[j]), cos_d, sin_d)
            lambda_init = 0.8 - 0.6 * math.exp(-0.3 * i)
            lam = (jnp.exp(jnp.sum(c_lambda_q1[j] * c_lambda_k1[j]).astype(jnp.float32))
                   - jnp.exp(jnp.sum(c_lambda_q2[j] * c_lambda_k2[j]).astype(jnp.float32))
                   + lambda_init)
            y = differential_attention(q[:, :, 0], q[:, :, 1], k[:, :, 0], k[:, :, 1],
                                       split_heads(v, N_HEADS), lam)
            y = rms_norm(y, c_subln[j]) * (1.0 - lambda_init)
        y = merge_heads(y) * jax.nn.silu(z)
        x = x + y @ w_out[i]
    return x
```

```python
import functools
import math

import jax
import jax.numpy as jnp
from jax import lax
from jax.experimental import pallas as pl
from jax.experimental.pallas import tpu as pltpu

N_HEADS = 8
HEAD_DIM = 128
ROPE_THETA = 10000.0
NORM_EPS = 1e-6
MOBA_BLOCK = 256
MOBA_TOPK = 3
DIFF_SUB_DIM = HEAD_DIM // 2

BLK = 256
LANES = 128
LOG2E = 1.4426950408889634
NEG_INF = float("-inf")
VMEM_LIMIT_BYTES = 56 * 1024 * 1024

MOBA, FOX, DIFF = 0, 1, 2


def _dot(a, b):
    return jnp.dot(a, b, preferred_element_type=jnp.float32)


def _head_norm_rope(kind, xh, gain, tabs, scale):
    sq = xh * xh
    if kind == DIFF:
        lane = lax.broadcasted_iota(jnp.int32, xh.shape, 1)
        lo = lane < DIFF_SUB_DIM
        s_lo = jnp.sum(jnp.where(lo, sq, 0.0), axis=-1, keepdims=True)
        s_hi = jnp.sum(jnp.where(lo, 0.0, sq), axis=-1, keepdims=True)
        ms = jnp.where(lo, s_lo, s_hi) * (1.0 / DIFF_SUB_DIM)
    else:
        ms = jnp.sum(sq, axis=-1, keepdims=True) * (1.0 / HEAD_DIM)
    y = xh * lax.rsqrt(ms + NORM_EPS) * gain
    if kind == MOBA:
        cos_t, sin_t = tabs
        y = y * cos_t + pltpu.roll(y, HEAD_DIM // 2, axis=1) * sin_t
    elif kind == DIFF:
        cos_t, s_up, s_dn = tabs
        y = (y * cos_t + pltpu.roll(y, HEAD_DIM - DIFF_SUB_DIM // 2, axis=1) * s_up
             + pltpu.roll(y, DIFF_SUB_DIM // 2, axis=1) * s_dn)
    if scale != 1.0:
        y = y * scale
    return y


def _proj_kernel(kind, tm, width, *refs):
    nsub = tm // BLK
    if kind == MOBA:
        (x_ref, ng_ref, w_ref, qg_ref, kg_ref, cos_ref, sin_ref,
         qT_ref, k_ref, vT_ref, sz_ref, hdn_ref) = refs
        tabs = (cos_ref[...], sin_ref[...])
    elif kind == FOX:
        (x_ref, ng_ref, w_ref, qg_ref, kg_ref, fb_ref,
         qT_ref, k_ref, vT_ref, sz_ref, c_ref, hdn_ref, carry_ref) = refs
        tabs = None
    else:
        (x_ref, ng_ref, w_ref, qg_ref, kg_ref, cos_ref, sup_ref, sdn_ref,
         qT_ref, qTb_ref, k_ref, vT_ref, sz_ref, hdn_ref) = refs
        tabs = (cos_ref[...], sup_ref[...], sdn_ref[...])

    x = x_ref[0]
    ms = jnp.mean(x * x, axis=-1, keepdims=True)
    hdn_ref[...] = (x * lax.rsqrt(ms + NORM_EPS) * ng_ref[...]).astype(jnp.bfloat16)

    sub_dim = DIFF_SUB_DIM if kind == DIFF else HEAD_DIM
    q_scale = (sub_dim ** -0.5) * LOG2E
    pair = 2 * HEAD_DIM

    def store_T(ref, h, val):
        vt = val.T
        for t in range(nsub):
            ref[0, h, t] = vt[:, t * BLK:(t + 1) * BLK].astype(jnp.bfloat16)

    for p in range(N_HEADS // 2):
        rq = _dot(hdn_ref[...], w_ref[:, p * pair:(p + 1) * pair])
        rk = _dot(hdn_ref[...], w_ref[:, width + p * pair:width + (p + 1) * pair])
        rv = _dot(hdn_ref[...], w_ref[:, 2 * width + p * pair:2 * width + (p + 1) * pair])
        for half in range(2):
            h = 2 * p + half
            sl = slice(half * HEAD_DIM, (half + 1) * HEAD_DIM)
            q = _head_norm_rope(kind, rq[:, sl], qg_ref[...], tabs, q_scale)
            kk = _head_norm_rope(kind, rk[:, sl], kg_ref[...], tabs, 1.0)
            if kind == DIFF:
                lane = lax.broadcasted_iota(jnp.int32, q.shape, 1)
                lo = lane < DIFF_SUB_DIM
                store_T(qT_ref, h, jnp.where(lo, q, 0.0))
                store_T(qTb_ref, h, jnp.where(lo, 0.0, q))
            else:
                store_T(qT_ref, h, q)
            for t in range(nsub):
                k_ref[0, h, t] = kk[t * BLK:(t + 1) * BLK, :].astype(jnp.bfloat16)
            store_T(vT_ref, h, rv[:, sl])

    zc = 512
    for c in range(width // zc):
        rz = _dot(hdn_ref[...], w_ref[:, 3 * width + c * zc:3 * width + (c + 1) * zc])
        sz_ref[0, :, c * zc:(c + 1) * zc] = (rz * (1.0 / (1.0 + jnp.exp(-rz)))).astype(jnp.bfloat16)

    if kind == FOX:
        fl = _dot(hdn_ref[...], w_ref[:, 4 * width:4 * width + LANES]) + fb_ref[...]
        logf = jnp.minimum(fl, 0.0) - jnp.log(1.0 + jnp.exp(-jnp.abs(fl)))
        row = lax.broadcasted_iota(jnp.int32, (tm, tm), 0)
        col = lax.broadcasted_iota(jnp.int32, (tm, tm), 1)
        tril = jnp.where(row >= col, 1.0, 0.0).astype(jnp.bfloat16)
        hi = logf.astype(jnp.bfloat16)
        r1 = logf - hi.astype(jnp.float32)
        mid = r1.astype(jnp.bfloat16)
        lo_ = (r1 - mid.astype(jnp.float32)).astype(jnp.bfloat16)
        csum = _dot(tril, hi) + _dot(tril, mid) + _dot(tril, lo_)

        @pl.when(pl.program_id(1) == 0)
        def _():
            carry_ref[...] = jnp.zeros_like(carry_ref)

        c_tile = csum + carry_ref[0:1, :]
        carry_ref[0:1, :] = c_tile[tm - 1:tm, :]
        cT = (c_tile * LOG2E).T
        for h in range(N_HEADS):
            for t in range(nsub):
                c_ref[0, h, t] = cT[h:h + 1, t * BLK:(t + 1) * BLK]


def _project(kind, x, ng, w, qg, kg, extras, tm=512):
    b, s, d = x.shape
    width = N_HEADS * HEAD_DIM
    nblk = s // BLK
    nsub = tm // BLK
    grid = (b, s // tm)

    def const(shape):
        return pl.BlockSpec(shape, lambda bi, si: (0,) * len(shape))

    tab_spec = pl.BlockSpec((tm, HEAD_DIM), lambda bi, si: (si, 0))
    in_specs = [pl.BlockSpec((1, tm, d), lambda bi, si: (bi, si, 0)),
                const((1, d)), const(w.shape), const((1, HEAD_DIM)), const((1, HEAD_DIM))]
    if kind == MOBA:
        in_specs += [tab_spec, tab_spec]
    elif kind == FOX:
        in_specs += [const((1, LANES))]
    else:
        in_specs += [tab_spec, tab_spec, tab_spec]

    t_shape = jax.ShapeDtypeStruct((b, N_HEADS, nblk, HEAD_DIM, BLK), jnp.bfloat16)
    t_spec = pl.BlockSpec((1, N_HEADS, nsub, HEAD_DIM, BLK), lambda bi, si: (bi, 0, si, 0, 0))
    k_shape = jax.ShapeDtypeStruct((b, N_HEADS, nblk, BLK, HEAD_DIM), jnp.bfloat16)
    k_spec = pl.BlockSpec((1, N_HEADS, nsub, BLK, HEAD_DIM), lambda bi, si: (bi, 0, si, 0, 0))
    sz_shape = jax.ShapeDtypeStruct((b, s, width), jnp.bfloat16)
    sz_spec = pl.BlockSpec((1, tm, width), lambda bi, si: (bi, si, 0))

    out_shape = [t_shape] + ([t_shape] if kind == DIFF else []) + [k_shape, t_shape, sz_shape]
    out_specs = [t_spec] + ([t_spec] if kind == DIFF else []) + [k_spec, t_spec, sz_spec]
    scratch = [pltpu.VMEM((tm, d), jnp.bfloat16)]
    if kind == FOX:
        out_shape.append(jax.ShapeDtypeStruct((b, N_HEADS, nblk, 1, BLK), jnp.float32))
        out_specs.append(pl.BlockSpec((1, N_HEADS, nsub, 1, BLK), lambda bi, si: (bi, 0, si, 0, 0)))
        scratch.append(pltpu.VMEM((8, LANES), jnp.float32))

    return pl.pallas_call(
        functools.partial(_proj_kernel, kind, tm, width),
        grid=grid, in_specs=in_specs, out_specs=out_specs, out_shape=out_shape,
        scratch_shapes=scratch,
        compiler_params=pltpu.CompilerParams(
            dimension_semantics=("arbitrary", "arbitrary"), vmem_limit_bytes=VMEM_LIMIT_BYTES),
        name=f"proj_{('moba', 'fox', 'diff')[kind]}",
    )(x, ng, w, qg, kg, *extras)


def _attn_kernel(kind, nblk, lambda_init, *refs):
    if kind == MOBA:
        qT_ref, k_ref, vT_ref, o_ref, acc_ref, bias_ref = refs
        q_refs = (qT_ref,)
    elif kind == FOX:
        qT_ref, k_ref, vT_ref, c_ref, o_ref, acc_ref, ck_ref = refs
        q_refs = (qT_ref,)
    else:
        qT_ref, qTb_ref, k_ref, vT_ref, lam_ref, sg_ref, o_ref, acc_ref = refs
        q_refs = (qT_ref, qTb_ref)
    nvar = len(q_refs)

    if kind == MOBA:
        kmean = jnp.concatenate(
            [jnp.sum(k_ref[0, 0, j].astype(jnp.float32), axis=0, keepdims=True) for j in range(nblk)], axis=0)
        kmean = (kmean * (1.0 / MOBA_BLOCK)).astype(jnp.bfloat16)
        row = lax.broadcasted_iota(jnp.int32, (nblk, BLK), 0)
        for t in range(nblk):
            gate = _dot(kmean, qT_ref[0, 0, t])
            past = row < t
            gm = jnp.where(past, gate, NEG_INF)
            cnt = jnp.zeros((nblk, BLK), jnp.int32)
            for r in range(1, nblk):
                other = pltpu.roll(gm, r, axis=0)
                tie_first = jnp.where(row >= r, 1, 0)
                cnt = cnt + jnp.where(other > gm, 1, jnp.where(other == gm, tie_first, 0))
            bias = jnp.where(past, jnp.where(cnt < MOBA_TOPK, 0.0, NEG_INF), NEG_INF)
            for j in range(nblk):
                bias_ref[t, j] = bias[j:j + 1, :]
    elif kind == FOX:
        for j in range(nblk):
            ck_ref[j] = jnp.broadcast_to(c_ref[0, 0, j], (BLK, BLK)).T

    krow = lax.broadcasted_iota(jnp.int32, (BLK, BLK), 0)
    qcol = lax.broadcasted_iota(jnp.int32, (BLK, BLK), 1)
    causal = krow <= qcol

    def q_tile(qi, _):
        qs = [r[0, 0, qi] for r in q_refs]
        kd = k_ref[0, 0, qi]
        vd = vT_ref[0, 0, qi]
        ms, ls = [], []
        for v in range(nvar):
            s = _dot(kd, qs[v])
            if kind == FOX:
                s = s - ck_ref[qi]
            s = jnp.where(causal, s, NEG_INF)
            m = jnp.max(s, axis=0, keepdims=True)
            if kind == FOX:
                cq = c_ref[0, 0, qi]
                p = jnp.exp2(s + (cq - (m + cq)))
                m = m + cq
            else:
                p = jnp.exp2(s - m)
            ms.append(m)
            ls.append(jnp.sum(p, axis=0, keepdims=True))
            acc_ref[v] = _dot(vd, p.astype(jnp.bfloat16))

        def k_block(j, carry):
            ms, ls = carry
            kb = k_ref[0, 0, j]
            vb = vT_ref[0, 0, j]
            new_m, new_l = [], []
            for v in range(nvar):
                s = _dot(kb, qs[v])
                if kind == FOX:
                    s = s - ck_ref[j]
                elif kind == MOBA:
                    s = s + bias_ref[qi, j]
                mb = jnp.max(s, axis=0, keepdims=True)
                if kind == FOX:
                    cq = c_ref[0, 0, qi]
                    m_new = jnp.maximum(ms[v], mb + cq)
                    p = jnp.exp2(s + (cq - m_new))
                else:
                    m_new = jnp.maximum(ms[v], mb)
                    p = jnp.exp2(s - m_new)
                alpha = jnp.exp2(ms[v] - m_new)
                new_m.append(m_new)
                new_l.append(alpha * ls[v] + jnp.sum(p, axis=0, keepdims=True))
                acc_ref[v] = alpha * acc_ref[v] + _dot(vb, p.astype(jnp.bfloat16))
            return tuple(new_m), tuple(new_l)

        ms, ls = lax.fori_loop(0, qi, k_block, (tuple(ms), tuple(ls)))

        if kind == DIFF:
            lam = (jnp.exp(jnp.sum(lam_ref[0:1, :] * lam_ref[1:2, :], axis=1, keepdims=True))
                   - jnp.exp(jnp.sum(lam_ref[2:3, :] * lam_ref[3:4, :], axis=1, keepdims=True))
                   + lambda_init)
            yT = acc_ref[0] * (1.0 / ls[0]) - lam * (acc_ref[1] * (1.0 / ls[1]))
            y = yT.T
            msq = jnp.mean(y * y, axis=-1, keepdims=True)
            y = y * lax.rsqrt(msq + NORM_EPS) * sg_ref[...] * (1.0 - lambda_init)
        else:
            y = (acc_ref[0] * (1.0 / ls[0])).T
        o_ref[0, pl.ds(pl.multiple_of(qi * BLK, BLK), BLK), :] = y.astype(jnp.bfloat16)
        return 0

    lax.fori_loop(0, nblk, q_tile, 0)


def _attention(kind, qTs, k, vT, extras, lambda_init=0.0):
    b, h, nblk = k.shape[:3]
    s = nblk * BLK
    width = h * HEAD_DIM
    t_spec = pl.BlockSpec((1, 1, nblk, HEAD_DIM, BLK), lambda bi, hi: (bi, hi, 0, 0, 0))
    k_spec = pl.BlockSpec((1, 1, nblk, BLK, HEAD_DIM), lambda bi, hi: (bi, hi, 0, 0, 0))
    in_specs = [t_spec] * len(qTs) + [k_spec, t_spec]
    scratch = [pltpu.VMEM((len(qTs), HEAD_DIM, BLK), jnp.float32)]
    if kind == MOBA:
        scratch.append(pltpu.VMEM((nblk, nblk, 1, BLK), jnp.float32))
    elif kind == FOX:
        in_specs.append(pl.BlockSpec((1, 1, nblk, 1, BLK), lambda bi, hi: (bi, hi, 0, 0, 0)))
        scratch.append(pltpu.VMEM((nblk, BLK, BLK), jnp.float32))
    else:
        in_specs += [pl.BlockSpec((4, DIFF_SUB_DIM), lambda bi, hi: (0, 0)),
                     pl.BlockSpec((1, HEAD_DIM), lambda bi, hi: (0, 0))]
    return pl.pallas_call(
        functools.partial(_attn_kernel, kind, nblk, lambda_init),
        grid=(b, h), in_specs=in_specs,
        out_specs=pl.BlockSpec((1, s, HEAD_DIM), lambda bi, hi: (bi, 0, hi)),
        out_shape=jax.ShapeDtypeStruct((b, s, width), jnp.bfloat16),
        scratch_shapes=scratch,
        compiler_params=pltpu.CompilerParams(
            dimension_semantics=("arbitrary", "arbitrary"), vmem_limit_bytes=VMEM_LIMIT_BYTES),
        name=f"attn_{('moba', 'fox', 'diff')[kind]}",
    )(*qTs, k, vT, *extras)


def _out_kernel(y_ref, sz_ref, x_ref, w_ref, o_ref):
    g = (y_ref[...].astype(jnp.float32) * sz_ref[...].astype(jnp.float32)).astype(jnp.bfloat16)
    o_ref[...] = x_ref[...] + _dot(g, w_ref[...])


def _out_project(y, sz, x, w, tm=1024):
    t, d = x.shape
    width = y.shape[-1]
    return pl.pallas_call(
        _out_kernel, grid=(t // tm,),
        in_specs=[pl.BlockSpec((tm, width), lambda i: (i, 0)), pl.BlockSpec((tm, width), lambda i: (i, 0)),
                  pl.BlockSpec((tm, d), lambda i: (i, 0)), pl.BlockSpec((width, d), lambda i: (0, 0))],
        out_specs=pl.BlockSpec((tm, d), lambda i: (i, 0)),
        out_shape=jax.ShapeDtypeStruct((t, d), jnp.float32),
        compiler_params=pltpu.CompilerParams(
            dimension_semantics=("arbitrary",), vmem_limit_bytes=VMEM_LIMIT_BYTES),
        name="out_proj",
    )(y, sz, x, w)


def _rope_tables(seq, dim):
    inv = 1.0 / (ROPE_THETA ** (jnp.arange(0, dim, 2, dtype=jnp.float32) / dim))
    ang = jnp.arange(seq, dtype=jnp.float32)[:, None] * inv[None, :]
    return jnp.cos(ang), jnp.sin(ang)


def kernel(x, norm_g, w_out, a_w_in, a_q_norm, a_k_norm, b_w_in, b_f_bias, b_q_norm, b_k_norm,
           c_w_in, c_q_norm, c_k_norm, c_lambda_q1, c_lambda_k1, c_lambda_q2, c_lambda_k2, c_subln):
    b, s, d = x.shape
    depth = norm_g.shape[0]
    width = N_HEADS * HEAD_DIM
    assert s % BLK == 0 and BLK == MOBA_BLOCK and d == width

    cos_h, sin_h = _rope_tables(s, HEAD_DIM)
    moba_tabs = (jnp.concatenate([cos_h, cos_h], axis=1), jnp.concatenate([-sin_h, sin_h], axis=1))
    cos_d, sin_d = _rope_tables(s, DIFF_SUB_DIM)
    zero = jnp.zeros_like(sin_d)
    diff_tabs = (jnp.concatenate([cos_d] * 4, axis=1),
                 jnp.concatenate([-sin_d, zero, -sin_d, zero], axis=1),
                 jnp.concatenate([zero, sin_d, zero, sin_d], axis=1))

    for i in range(depth):
        kind, j = i % 3, i // 3
        ng = norm_g[i][None, :]
        if kind == MOBA:
            w = a_w_in[j].astype(jnp.bfloat16)
            qT, k, vT, sz = _project(MOBA, x, ng, w, a_q_norm[j][None, :], a_k_norm[j][None, :], moba_tabs)
            y = _attention(MOBA, (qT,), k, vT, ())
        elif kind == FOX:
            w = jnp.pad(b_w_in[j], ((0, 0), (0, LANES - N_HEADS))).astype(jnp.bfloat16)
            fb = jnp.pad(b_f_bias[j], (0, LANES - N_HEADS))[None, :]
            qT, k, vT, sz, c = _project(FOX, x, ng, w, b_q_norm[j][None, :], b_k_norm[j][None, :], (fb,))
            y = _attention(FOX, (qT,), k, vT, (c,))
        else:
            w = c_w_in[j].astype(jnp.bfloat16)
            qg = jnp.concatenate([c_q_norm[j]] * 2)[None, :]
            kg = jnp.concatenate([c_k_norm[j]] * 2)[None, :]
            qTa, qTb, k, vT, sz = _project(DIFF, x, ng, w, qg, kg, diff_tabs)
            lam = jnp.stack([c_lambda_q1[j], c_lambda_k1[j], c_lambda_q2[j], c_lambda_k2[j]])
            lambda_init = 0.8 - 0.6 * math.exp(-0.3 * i)
            y = _attention(DIFF, (qTa, qTb), k, vT, (lam, c_subln[j][None, :]), lambda_init)
        x = _out_project(y.reshape(b * s, width), sz.reshape(b * s, width), x.reshape(b * s, d),
                         w_out[i].astype(jnp.bfloat16)).reshape(b, s, d)
    return x
```

```python
import functools
import math

import jax
import jax.numpy as jnp
from jax import lax
from jax.experimental import pallas as pl
from jax.experimental.pallas import tpu as pltpu

N_HEADS = 8
HEAD_DIM = 128
ROPE_THETA = 10000.0
NORM_EPS = 1e-6
MOBA_BLOCK = 256
MOBA_TOPK = 3
DIFF_SUB_DIM = HEAD_DIM // 2

BLK = 256
QK_LOOKAHEAD = (3, 3, 2)
LANES = 128
LOG2E = 1.4426950408889634
NEG_INF = float("-inf")
VMEM_LIMIT_BYTES = 56 * 1024 * 1024

MOBA, FOX, DIFF = 0, 1, 2
KIND_NAMES = ("moba", "fox", "diff")


def _dot(a, b):
    return jnp.dot(a, b, preferred_element_type=jnp.float32)


def _head_norm_rope(kind, xh, gain, tabs, scale):
    sq = xh * xh
    if kind == DIFF:
        lane = lax.broadcasted_iota(jnp.int32, xh.shape, 1)
        lo = lane < DIFF_SUB_DIM
        s_lo = jnp.sum(jnp.where(lo, sq, 0.0), axis=-1, keepdims=True)
        s_hi = jnp.sum(jnp.where(lo, 0.0, sq), axis=-1, keepdims=True)
        ms = jnp.where(lo, s_lo, s_hi) * (1.0 / DIFF_SUB_DIM)
    else:
        ms = jnp.sum(sq, axis=-1, keepdims=True) * (1.0 / HEAD_DIM)
    y = xh * lax.rsqrt(ms + NORM_EPS) * gain
    if kind == MOBA:
        cos_t, sin_t = tabs
        y = y * cos_t + pltpu.roll(y, HEAD_DIM // 2, axis=1) * sin_t
    elif kind == DIFF:
        cos_t, s_up, s_dn = tabs
        y = (y * cos_t + pltpu.roll(y, HEAD_DIM - DIFF_SUB_DIM // 2, axis=1) * s_up
             + pltpu.roll(y, DIFF_SUB_DIM // 2, axis=1) * s_dn)
    if scale != 1.0:
        y = y * scale
    return y


def _proj_kernel(kind, tm, width, *refs):
    if kind == MOBA:
        (x_ref, ng_ref, w_ref, qg_ref, kg_ref, cos_ref, sin_ref,
         qT_ref, k_ref, vT_ref, sz_ref, hdn_ref) = refs
        tabs = (cos_ref[...], sin_ref[...])
    elif kind == FOX:
        (x_ref, ng_ref, w_ref, qg_ref, kg_ref, fb_ref,
         qT_ref, k_ref, vT_ref, sz_ref, c_ref, hdn_ref, carry_ref) = refs
        tabs = None
    else:
        (x_ref, ng_ref, w_ref, qg_ref, kg_ref, cos_ref, sup_ref, sdn_ref,
         qT_ref, qTb_ref, k_ref, vT_ref, sz_ref, hdn_ref) = refs
        tabs = (cos_ref[...], sup_ref[...], sdn_ref[...])

    x = x_ref[0]
    ms = jnp.mean(x * x, axis=-1, keepdims=True)
    hdn_ref[...] = (x * lax.rsqrt(ms + NORM_EPS) * ng_ref[...]).astype(jnp.bfloat16)

    sub_dim = DIFF_SUB_DIM if kind == DIFF else HEAD_DIM
    q_scale = (sub_dim ** -0.5) * LOG2E
    pair = 2 * HEAD_DIM

    for p in range(N_HEADS // 2):
        rq = _dot(hdn_ref[...], w_ref[:, p * pair:(p + 1) * pair])
        rk = _dot(hdn_ref[...], w_ref[:, width + p * pair:width + (p + 1) * pair])
        rv = _dot(hdn_ref[...], w_ref[:, 2 * width + p * pair:2 * width + (p + 1) * pair])
        for half in range(2):
            h = 2 * p + half
            sl = slice(half * HEAD_DIM, (half + 1) * HEAD_DIM)
            q = _head_norm_rope(kind, rq[:, sl], qg_ref[...], tabs, q_scale)
            kk = _head_norm_rope(kind, rk[:, sl], kg_ref[...], tabs, 1.0)
            if kind == DIFF:
                lane = lax.broadcasted_iota(jnp.int32, q.shape, 1)
                lo = lane < DIFF_SUB_DIM
                qT_ref[0, h] = jnp.where(lo, q, 0.0).T.astype(jnp.bfloat16)
                qTb_ref[0, h] = jnp.where(lo, 0.0, q).T.astype(jnp.bfloat16)
            else:
                qT_ref[0, h] = q.T.astype(jnp.bfloat16)
            k_ref[0, h] = kk.astype(jnp.bfloat16)
            vT_ref[0, h] = rv[:, sl].T.astype(jnp.bfloat16)

    zc = 512
    for c in range(width // zc):
        rz = _dot(hdn_ref[...], w_ref[:, 3 * width + c * zc:3 * width + (c + 1) * zc])
        sz_ref[0, :, c * zc:(c + 1) * zc] = (rz * (1.0 / (1.0 + jnp.exp(-rz)))).astype(jnp.bfloat16)

    if kind == FOX:
        fl = _dot(hdn_ref[...], w_ref[:, 4 * width:4 * width + LANES]) + fb_ref[...]
        logf = jnp.minimum(fl, 0.0) - jnp.log(1.0 + jnp.exp(-jnp.abs(fl)))
        row = lax.broadcasted_iota(jnp.int32, (tm, tm), 0)
        col = lax.broadcasted_iota(jnp.int32, (tm, tm), 1)
        tril = jnp.where(row >= col, 1.0, 0.0).astype(jnp.bfloat16)
        hi = logf.astype(jnp.bfloat16)
        r1 = logf - hi.astype(jnp.float32)
        mid = r1.astype(jnp.bfloat16)
        lo_ = (r1 - mid.astype(jnp.float32)).astype(jnp.bfloat16)
        csum = _dot(tril, hi) + _dot(tril, mid) + _dot(tril, lo_)

        @pl.when(pl.program_id(1) == 0)
        def _():
            carry_ref[...] = jnp.zeros_like(carry_ref)

        c_tile = csum + carry_ref[0:1, :]
        carry_ref[0:1, :] = c_tile[tm - 1:tm, :]
        cT = (c_tile * LOG2E).T
        for h in range(N_HEADS):
            c_ref[0, h] = cT[h:h + 1, :]


def _project(kind, x, ng, w, qg, kg, extras, tm=512):
    b, s, d = x.shape
    width = N_HEADS * HEAD_DIM
    grid = (b, s // tm)

    def const(shape):
        return pl.BlockSpec(shape, lambda bi, si: (0,) * len(shape))

    tab_spec = pl.BlockSpec((tm, HEAD_DIM), lambda bi, si: (si, 0))
    in_specs = [pl.BlockSpec((1, tm, d), lambda bi, si: (bi, si, 0)),
                const((1, d)), const(w.shape), const((1, HEAD_DIM)), const((1, HEAD_DIM))]
    if kind == MOBA:
        in_specs += [tab_spec, tab_spec]
    elif kind == FOX:
        in_specs += [const((1, LANES))]
    else:
        in_specs += [tab_spec, tab_spec, tab_spec]

    t_shape = jax.ShapeDtypeStruct((b, N_HEADS, HEAD_DIM, s), jnp.bfloat16)
    t_spec = pl.BlockSpec((1, N_HEADS, HEAD_DIM, tm), lambda bi, si: (bi, 0, 0, si))
    k_shape = jax.ShapeDtypeStruct((b, N_HEADS, s, HEAD_DIM), jnp.bfloat16)
    k_spec = pl.BlockSpec((1, N_HEADS, tm, HEAD_DIM), lambda bi, si: (bi, 0, si, 0))
    sz_shape = jax.ShapeDtypeStruct((b, s, width), jnp.bfloat16)
    sz_spec = pl.BlockSpec((1, tm, width), lambda bi, si: (bi, si, 0))

    out_shape = [t_shape] + ([t_shape] if kind == DIFF else []) + [k_shape, t_shape, sz_shape]
    out_specs = [t_spec] + ([t_spec] if kind == DIFF else []) + [k_spec, t_spec, sz_spec]
    scratch = [pltpu.VMEM((tm, d), jnp.bfloat16)]
    if kind == FOX:
        out_shape.append(jax.ShapeDtypeStruct((b, N_HEADS, 1, s), jnp.float32))
        out_specs.append(pl.BlockSpec((1, N_HEADS, 1, tm), lambda bi, si: (bi, 0, 0, si)))
        scratch.append(pltpu.VMEM((8, LANES), jnp.float32))

    return pl.pallas_call(
        functools.partial(_proj_kernel, kind, tm, width),
        grid=grid, in_specs=in_specs, out_specs=out_specs, out_shape=out_shape,
        scratch_shapes=scratch,
        compiler_params=pltpu.CompilerParams(
            dimension_semantics=("arbitrary", "arbitrary"), vmem_limit_bytes=VMEM_LIMIT_BYTES),
        name=f"proj_{KIND_NAMES[kind]}",
    )(x, ng, w, qg, kg, *extras)


def _attn_kernel(kind, nblk, lambda_init, *refs):
    if kind == MOBA:
        qT_ref, k_ref, vT_ref, o_ref, s_ref, p_ref, bias_ref = refs
        q_refs = (qT_ref,)
    elif kind == FOX:
        qT_ref, k_ref, vT_ref, c_ref, o_ref, s_ref, p_ref, ck_ref = refs
        q_refs = (qT_ref,)
    else:
        qT_ref, qTb_ref, k_ref, vT_ref, lam_ref, sg_ref, o_ref, s_ref, p_ref = refs
        q_refs = (qT_ref, qTb_ref)
    nvar = len(q_refs)

    if kind == MOBA:
        kmean = jnp.concatenate(
            [jnp.sum(k_ref[0, 0, j * BLK:(j + 1) * BLK, :].astype(jnp.float32), axis=0, keepdims=True)
             for j in range(nblk)], axis=0)
        kmean = (kmean * (1.0 / MOBA_BLOCK)).astype(jnp.bfloat16)
        row = lax.broadcasted_iota(jnp.int32, (nblk, BLK), 0)
        for t in range(1, nblk):
            gate = _dot(kmean, qT_ref[0, 0, :, t * BLK:(t + 1) * BLK])
            past = row < t
            gm = jnp.where(past, gate, NEG_INF)
            cnt = jnp.zeros((nblk, BLK), jnp.int32)
            for r in range(1, nblk):
                other = pltpu.roll(gm, r, axis=0)
                tie_first = jnp.where(row >= r, 1, 0)
                cnt = cnt + jnp.where(other > gm, 1, jnp.where(other == gm, tie_first, 0))
            bias_ref[t] = jnp.where(past, jnp.where(cnt < MOBA_TOPK, 0.0, NEG_INF), NEG_INF)
    elif kind == FOX:
        for j in range(nblk):
            ck_ref[j] = jnp.broadcast_to(c_ref[0, 0, :, j * BLK:(j + 1) * BLK], (BLK, BLK)).T

    krow = lax.broadcasted_iota(jnp.int32, (BLK, BLK), 0)
    qcol = lax.broadcasted_iota(jnp.int32, (BLK, BLK), 1)
    causal = krow <= qcol

    def base_slot(qi, v):
        return nvar * qi * (qi + 1) // 2 + v * (qi + 1)

    def scores(qi):
        nk = qi + 1
        q_sl = slice(qi * BLK, (qi + 1) * BLK)
        k_all = k_ref[0, 0, 0:nk * BLK, :]
        shifts = []
        for v in range(nvar):
            base = base_slot(qi, v)
            s = _dot(k_all, q_refs[v][0, 0, :, q_sl])
            m = None
            for j in range(nk):
                sj = s[j * BLK:(j + 1) * BLK]
                if kind == FOX:
                    sj = sj - ck_ref[j]
                if j == qi:
                    sj = jnp.where(causal, sj, NEG_INF)
                s_ref[base + j] = sj
                cm = jnp.max(sj, axis=0, keepdims=True)
                if kind == MOBA and j < qi:
                    cm = cm + bias_ref[qi, j:j + 1, :]
                m = cm if m is None else jnp.maximum(m, cm)
            if kind == FOX:
                cq = c_ref[0, 0, :, q_sl]
                shifts.append(cq - (m + cq))
            else:
                shifts.append(-m)
        return shifts

    def finish(qi, shifts):
        nk = qi + 1
        q_sl = slice(qi * BLK, (qi + 1) * BLK)
        v_all = vT_ref[0, 0, :, 0:nk * BLK]
        outs = []
        for v in range(nvar):
            base = base_slot(qi, v)
            l = None
            for j in range(nk):
                sh = shifts[v] + bias_ref[qi, j:j + 1, :] if (kind == MOBA and j < qi) else shifts[v]
                pj = jnp.exp2(s_ref[base + j] + sh)
                lj = jnp.sum(pj, axis=0, keepdims=True)
                l = lj if l is None else l + lj
                p_ref[(base + j) * BLK:(base + j + 1) * BLK, :] = pj.astype(jnp.bfloat16)
            acc = _dot(v_all, p_ref[base * BLK:(base + nk) * BLK, :])
            outs.append(acc * (1.0 / l))
        if kind == DIFF:
            lam = (jnp.exp(jnp.sum(lam_ref[0:1, :] * lam_ref[1:2, :], axis=1, keepdims=True))
                   - jnp.exp(jnp.sum(lam_ref[2:3, :] * lam_ref[3:4, :], axis=1, keepdims=True))
                   + lambda_init)
            y = (outs[0] - lam * outs[1]).T
            msq = jnp.mean(y * y, axis=-1, keepdims=True)
            y = y * lax.rsqrt(msq + NORM_EPS) * sg_ref[...] * (1.0 - lambda_init)
        else:
            y = outs[0].T
        o_ref[0, q_sl, :] = y.astype(jnp.bfloat16)

    pending = {}
    lookahead = QK_LOOKAHEAD[kind]
    for step in range(nblk + lookahead):
        if step < nblk:
            pending[step] = scores(step)
        done = step - lookahead
        if done >= 0:
            finish(done, pending.pop(done))


def _attention(kind, qTs, k, vT, extras, lambda_init=0.0):
    b, h, s, _ = k.shape
    nblk = s // BLK
    width = h * HEAD_DIM
    nslots = len(qTs) * nblk * (nblk + 1) // 2
    t_spec = pl.BlockSpec((1, 1, HEAD_DIM, s), lambda bi, hi: (bi, hi, 0, 0))
    k_spec = pl.BlockSpec((1, 1, s, HEAD_DIM), lambda bi, hi: (bi, hi, 0, 0))
    in_specs = [t_spec] * len(qTs) + [k_spec, t_spec]
    scratch = [pltpu.VMEM((nslots, BLK, BLK), jnp.float32), pltpu.VMEM((nslots * BLK, BLK), jnp.bfloat16)]
    if kind == MOBA:
        scratch.append(pltpu.VMEM((nblk, nblk, BLK), jnp.float32))
    elif kind == FOX:
        in_specs.append(pl.BlockSpec((1, 1, 1, s), lambda bi, hi: (bi, hi, 0, 0)))
        scratch.append(pltpu.VMEM((nblk, BLK, BLK), jnp.float32))
    else:
        in_specs += [pl.BlockSpec((4, DIFF_SUB_DIM), lambda bi, hi: (0, 0)),
                     pl.BlockSpec((1, HEAD_DIM), lambda bi, hi: (0, 0))]
    return pl.pallas_call(
        functools.partial(_attn_kernel, kind, nblk, lambda_init),
        grid=(b, h), in_specs=in_specs,
        out_specs=pl.BlockSpec((1, s, HEAD_DIM), lambda bi, hi: (bi, 0, hi)),
        out_shape=jax.ShapeDtypeStruct((b, s, width), jnp.bfloat16),
        scratch_shapes=scratch,
        compiler_params=pltpu.CompilerParams(
            dimension_semantics=("arbitrary", "arbitrary"), vmem_limit_bytes=VMEM_LIMIT_BYTES),
        name=f"attn_{KIND_NAMES[kind]}",
    )(*qTs, k, vT, *extras)


def _out_kernel(y_ref, sz_ref, x_ref, w_ref, o_ref):
    g = (y_ref[...].astype(jnp.float32) * sz_ref[...].astype(jnp.float32)).astype(jnp.bfloat16)
    o_ref[...] = x_ref[...] + _dot(g, w_ref[...])


def _out_project(y, sz, x, w, tm=1024):
    t, d = x.shape
    width = y.shape[-1]
    return pl.pallas_call(
        _out_kernel, grid=(t // tm,),
        in_specs=[pl.BlockSpec((tm, width), lambda i: (i, 0)), pl.BlockSpec((tm, width), lambda i: (i, 0)),
                  pl.BlockSpec((tm, d), lambda i: (i, 0)), pl.BlockSpec((width, d), lambda i: (0, 0))],
        out_specs=pl.BlockSpec((tm, d), lambda i: (i, 0)),
        out_shape=jax.ShapeDtypeStruct((t, d), jnp.float32),
        compiler_params=pltpu.CompilerParams(
            dimension_semantics=("arbitrary",), vmem_limit_bytes=VMEM_LIMIT_BYTES),
        name="out_proj",
    )(y, sz, x, w)


def _rope_tables(seq, dim):
    inv = 1.0 / (ROPE_THETA ** (jnp.arange(0, dim, 2, dtype=jnp.float32) / dim))
    ang = jnp.arange(seq, dtype=jnp.float32)[:, None] * inv[None, :]
    return jnp.cos(ang), jnp.sin(ang)


def kernel(x, norm_g, w_out, a_w_in, a_q_norm, a_k_norm, b_w_in, b_f_bias, b_q_norm, b_k_norm,
           c_w_in, c_q_norm, c_k_norm, c_lambda_q1, c_lambda_k1, c_lambda_q2, c_lambda_k2, c_subln):
    b, s, d = x.shape
    depth = norm_g.shape[0]
    width = N_HEADS * HEAD_DIM
    assert s % BLK == 0 and BLK == MOBA_BLOCK and d == width

    cos_h, sin_h = _rope_tables(s, HEAD_DIM)
    moba_tabs = (jnp.concatenate([cos_h, cos_h], axis=1), jnp.concatenate([-sin_h, sin_h], axis=1))
    cos_d, sin_d = _rope_tables(s, DIFF_SUB_DIM)
    zero = jnp.zeros_like(sin_d)
    diff_tabs = (jnp.concatenate([cos_d] * 4, axis=1),
                 jnp.concatenate([-sin_d, zero, -sin_d, zero], axis=1),
                 jnp.concatenate([zero, sin_d, zero, sin_d], axis=1))

    for i in range(depth):
        kind, j = i % 3, i // 3
        ng = norm_g[i][None, :]
        if kind == MOBA:
            w = a_w_in[j].astype(jnp.bfloat16)
            qT, k, vT, sz = _project(MOBA, x, ng, w, a_q_norm[j][None, :], a_k_norm[j][None, :], moba_tabs)
            y = _attention(MOBA, (qT,), k, vT, ())
        elif kind == FOX:
            w = jnp.pad(b_w_in[j], ((0, 0), (0, LANES - N_HEADS))).astype(jnp.bfloat16)
            fb = jnp.pad(b_f_bias[j], (0, LANES - N_HEADS))[None, :]
            qT, k, vT, sz, c = _project(FOX, x, ng, w, b_q_norm[j][None, :], b_k_norm[j][None, :], (fb,))
            y = _attention(FOX, (qT,), k, vT, (c,))
        else:
            w = c_w_in[j].astype(jnp.bfloat16)
            qg = jnp.concatenate([c_q_norm[j]] * 2)[None, :]
            kg = jnp.concatenate([c_k_norm[j]] * 2)[None, :]
            qTa, qTb, k, vT, sz = _project(DIFF, x, ng, w, qg, kg, diff_tabs)
            lam = jnp.stack([c_lambda_q1[j], c_lambda_k1[j], c_lambda_q2[j], c_lambda_k2[j]])
            lambda_init = 0.8 - 0.6 * math.exp(-0.3 * i)
            y = _attention(DIFF, (qTa, qTb), k, vT, (lam, c_subln[j][None, :]), lambda_init)
        x = _out_project(y.reshape(b * s, width), sz.reshape(b * s, width), x.reshape(b * s, d),
                         w_out[i].astype(jnp.bfloat16)).reshape(b, s, d)
    return x
```

```python
import functools
import math

import jax
import jax.numpy as jnp
from jax import lax
from jax.experimental import pallas as pl
from jax.experimental.pallas import tpu as pltpu

N_HEADS = 8
HEAD_DIM = 128
ROPE_THETA = 10000.0
NORM_EPS = 1e-6
MOBA_BLOCK = 256
MOBA_TOPK = 3
DIFF_SUB_DIM = HEAD_DIM // 2

BLK = 256
QK_LOOKAHEAD = (3, 3, 2)
ATTN_HEADS_PER_STEP = 4
LANES = 128
LOG2E = 1.4426950408889634
NEG_INF = float("-inf")
VMEM_LIMIT_BYTES = 56 * 1024 * 1024

MOBA, FOX, DIFF = 0, 1, 2
KIND_NAMES = ("moba", "fox", "diff")


def _dot(a, b):
    return jnp.dot(a, b, preferred_element_type=jnp.float32)


def _dot_nt(a, b):
    return lax.dot_general(a, b, (((1,), (1,)), ((), ())), preferred_element_type=jnp.float32)


def _head_norm_rope(kind, xh, ssq, gain, tabs, scale):
    norm_dim = DIFF_SUB_DIM if kind == DIFF else HEAD_DIM
    y = xh * lax.rsqrt(ssq * (1.0 / norm_dim) + NORM_EPS) * gain
    if kind == MOBA:
        cos_t, sin_t = tabs
        y = y * cos_t + pltpu.roll(y, HEAD_DIM // 2, axis=1) * sin_t
    elif kind == DIFF:
        cos_t, s_up, s_dn = tabs
        y = (y * cos_t + pltpu.roll(y, HEAD_DIM - DIFF_SUB_DIM // 2, axis=1) * s_up
             + pltpu.roll(y, DIFF_SUB_DIM // 2, axis=1) * s_dn)
    if scale != 1.0:
        y = y * scale
    return y


def _proj_kernel(kind, tm, width, *refs):
    if kind == MOBA:
        (x_ref, ng_ref, w_ref, qg_ref, kg_ref, cos_ref, sin_ref,
         q_ref, k_ref, v_ref, sz_ref, hdn_ref) = refs
        tabs = (cos_ref[...], sin_ref[...])
    elif kind == FOX:
        (x_ref, ng_ref, w_ref, qg_ref, kg_ref, fb_ref,
         q_ref, k_ref, v_ref, sz_ref, c_ref, hdn_ref, carry_ref) = refs
        tabs = None
    else:
        (x_ref, ng_ref, w_ref, qg_ref, kg_ref, cos_ref, sup_ref, sdn_ref,
         q_ref, qb_ref, k_ref, v_ref, sz_ref, hdn_ref) = refs
        tabs = (cos_ref[...], sup_ref[...], sdn_ref[...])

    if kind == FOX:
        @pl.when(pl.program_id(1) == 0)
        def _():
            carry_ref[...] = jnp.zeros_like(carry_ref)

    x = x_ref[0]
    ms = jnp.mean(x * x, axis=-1, keepdims=True)
    hdn_ref[...] = (x * lax.rsqrt(ms + NORM_EPS) * ng_ref[...]).astype(jnp.bfloat16)

    norm_dim = DIFF_SUB_DIM if kind == DIFF else HEAD_DIM
    q_scale = (norm_dim ** -0.5) * LOG2E
    pair = 2 * HEAD_DIM
    shift = norm_dim.bit_length() - 1
    gi = lax.broadcasted_iota(jnp.int32, (pair, pair), 0) >> shift
    gj = lax.broadcasted_iota(jnp.int32, (pair, pair), 1) >> shift
    group_ones = jnp.where(gi == gj, 1.0, 0.0).astype(jnp.bfloat16)

    def head_dots(p):
        return tuple(_dot(hdn_ref[...], w_ref[:, g * width + p * pair:g * width + (p + 1) * pair])
                     for g in range(3))

    def head_epilogue(p, rq, rk, rv):
        ssq_q = _dot((rq * rq).astype(jnp.bfloat16), group_ones)
        ssq_k = _dot((rk * rk).astype(jnp.bfloat16), group_ones)
        for half in range(2):
            h = 2 * p + half
            sl = slice(half * HEAD_DIM, (half + 1) * HEAD_DIM)
            q = _head_norm_rope(kind, rq[:, sl], ssq_q[:, sl], qg_ref[...], tabs, q_scale)
            kk = _head_norm_rope(kind, rk[:, sl], ssq_k[:, sl], kg_ref[...], tabs, 1.0)
            if kind == DIFF:
                lane = lax.broadcasted_iota(jnp.int32, q.shape, 1)
                lo = lane < DIFF_SUB_DIM
                q_ref[0, h] = jnp.where(lo, q, 0.0).astype(jnp.bfloat16)
                qb_ref[0, h] = jnp.where(lo, 0.0, q).astype(jnp.bfloat16)
            else:
                q_ref[0, h] = q.astype(jnp.bfloat16)
            k_ref[0, h] = kk.astype(jnp.bfloat16)
            v_ref[0, h] = rv[:, sl].astype(jnp.bfloat16)

    def forget_gates(fl):
        logf = jnp.minimum(fl, 0.0) - jnp.log(1.0 + jnp.exp(-jnp.abs(fl)))
        row = lax.broadcasted_iota(jnp.int32, (tm, tm), 0)
        col = lax.broadcasted_iota(jnp.int32, (tm, tm), 1)
        tril = jnp.where(row >= col, 1.0, 0.0).astype(jnp.bfloat16)
        hi = logf.astype(jnp.bfloat16)
        r1 = logf - hi.astype(jnp.float32)
        mid = r1.astype(jnp.bfloat16)
        lo_ = (r1 - mid.astype(jnp.float32)).astype(jnp.bfloat16)
        csum = _dot(tril, hi) + _dot(tril, mid) + _dot(tril, lo_)
        c_tile = csum + carry_ref[0:1, :]
        carry_ref[0:1, :] = c_tile[tm - 1:tm, :]
        cT = (c_tile * LOG2E).T
        for h in range(N_HEADS):
            c_ref[0, h] = cT[h:h + 1, :]

    if kind == FOX:
        fl = _dot(hdn_ref[...], w_ref[:, 4 * width:4 * width + LANES]) + fb_ref[...]
    prev = None
    for p in range(N_HEADS // 2):
        cur = head_dots(p)
        if kind == FOX and p == 0:
            forget_gates(fl)
        if prev is not None:
            head_epilogue(p - 1, *prev)
        prev = cur
    zc = width // 2
    rz = [None, None]
    rz[0] = _dot(hdn_ref[...], w_ref[:, 3 * width:3 * width + zc])
    head_epilogue(N_HEADS // 2 - 1, *prev)
    rz[1] = _dot(hdn_ref[...], w_ref[:, 3 * width + zc:4 * width])
    for c in range(2):
        sz_ref[0, :, c * zc:(c + 1) * zc] = (rz[c] * (1.0 / (1.0 + jnp.exp(-rz[c])))).astype(jnp.bfloat16)


def _project(kind, x, ng, w, qg, kg, extras, tm=512):
    b, s, d = x.shape
    width = N_HEADS * HEAD_DIM
    grid = (b, s // tm)

    def const(shape):
        return pl.BlockSpec(shape, lambda bi, si: (0,) * len(shape))

    tab_spec = pl.BlockSpec((tm, HEAD_DIM), lambda bi, si: (si, 0))
    in_specs = [pl.BlockSpec((1, tm, d), lambda bi, si: (bi, si, 0)),
                const((1, d)), const(w.shape), const((1, HEAD_DIM)), const((1, HEAD_DIM))]
    if kind == MOBA:
        in_specs += [tab_spec, tab_spec]
    elif kind == FOX:
        in_specs += [const((1, LANES))]
    else:
        in_specs += [tab_spec, tab_spec, tab_spec]

    h_shape = jax.ShapeDtypeStruct((b, N_HEADS, s, HEAD_DIM), jnp.bfloat16)
    h_spec = pl.BlockSpec((1, N_HEADS, tm, HEAD_DIM), lambda bi, si: (bi, 0, si, 0))
    sz_shape = jax.ShapeDtypeStruct((b, s, width), jnp.bfloat16)
    sz_spec = pl.BlockSpec((1, tm, width), lambda bi, si: (bi, si, 0))

    n_heads_out = 4 if kind == DIFF else 3
    out_shape = [h_shape] * n_heads_out + [sz_shape]
    out_specs = [h_spec] * n_heads_out + [sz_spec]
    scratch = [pltpu.VMEM((tm, d), jnp.bfloat16)]
    if kind == FOX:
        out_shape.append(jax.ShapeDtypeStruct((b, N_HEADS, 1, s), jnp.float32))
        out_specs.append(pl.BlockSpec((1, N_HEADS, 1, tm), lambda bi, si: (bi, 0, 0, si)))
        scratch.append(pltpu.VMEM((8, LANES), jnp.float32))

    return pl.pallas_call(
        functools.partial(_proj_kernel, kind, tm, width),
        grid=grid, in_specs=in_specs, out_specs=out_specs, out_shape=out_shape,
        scratch_shapes=scratch,
        compiler_params=pltpu.CompilerParams(
            dimension_semantics=("arbitrary", "arbitrary"), vmem_limit_bytes=VMEM_LIMIT_BYTES),
        name=f"proj_{KIND_NAMES[kind]}",
    )(x, ng, w, qg, kg, *extras)


def _attn_kernel(kind, nblk, heads, lambda_init, *refs):
    if kind == MOBA:
        q_ref, k_ref, v_ref, o_ref, s_ref, p_ref, vT_ref, bias_ref = refs
        q_refs = (q_ref,)
    elif kind == FOX:
        q_ref, k_ref, v_ref, c_ref, o_ref, s_ref, p_ref, vT_ref, ck_ref = refs
        q_refs = (q_ref,)
    else:
        q_ref, qb_ref, k_ref, v_ref, lam_ref, sg_ref, o_ref, s_ref, p_ref, vT_ref = refs
        q_refs = (q_ref, qb_ref)
    nvar = len(q_refs)
    lax.fori_loop(0, heads, functools.partial(
        _attn_head, kind, nblk, lambda_init, refs, q_refs), 0)


def _attn_head(kind, nblk, lambda_init, refs, q_refs, hh, carry):
    if kind == MOBA:
        _, k_ref, v_ref, o_ref, s_ref, p_ref, vT_ref, bias_ref = refs
    elif kind == FOX:
        _, k_ref, v_ref, c_ref, o_ref, s_ref, p_ref, vT_ref, ck_ref = refs
        cv = c_ref.at[0, hh]
    else:
        _, _, k_ref, v_ref, lam_ref, sg_ref, o_ref, s_ref, p_ref, vT_ref = refs
    nvar = len(q_refs)
    qv = [r.at[0, hh] for r in q_refs]
    kv = k_ref.at[0, hh]
    ov = o_ref.at[0, hh]

    vT_ref[...] = v_ref[0, hh].T

    if kind == MOBA:
        kmean = jnp.concatenate(
            [jnp.sum(kv[j * BLK:(j + 1) * BLK, :].astype(jnp.float32), axis=0, keepdims=True)
             for j in range(nblk)], axis=0)
        kmean = (kmean * (1.0 / MOBA_BLOCK)).astype(jnp.bfloat16)
        row = lax.broadcasted_iota(jnp.int32, (nblk, BLK), 0)
        for t in range(1, nblk):
            gate = _dot_nt(kmean, qv[0][t * BLK:(t + 1) * BLK, :])
            past = row < t
            gm = jnp.where(past, gate, NEG_INF)
            cnt = jnp.zeros((nblk, BLK), jnp.int32)
            for r in range(1, nblk):
                other = pltpu.roll(gm, r, axis=0)
                tie_first = jnp.where(row >= r, 1, 0)
                cnt = cnt + jnp.where(other > gm, 1, jnp.where(other == gm, tie_first, 0))
            bias_ref[t] = jnp.where(past, jnp.where(cnt < MOBA_TOPK, 0.0, NEG_INF), NEG_INF)
    elif kind == FOX:
        for j in range(nblk):
            ck_ref[j] = jnp.broadcast_to(cv[:, j * BLK:(j + 1) * BLK], (BLK, BLK)).T

    krow = lax.broadcasted_iota(jnp.int32, (BLK, BLK), 0)
    qcol = lax.broadcasted_iota(jnp.int32, (BLK, BLK), 1)
    causal = krow <= qcol

    def base_slot(qi, v):
        return nvar * qi * (qi + 1) // 2 + v * (qi + 1)

    def scores(qi):
        nk = qi + 1
        q_sl = slice(qi * BLK, (qi + 1) * BLK)
        k_all = kv[0:nk * BLK, :]
        shifts = []
        for v in range(nvar):
            base = base_slot(qi, v)
            s = _dot_nt(k_all, qv[v][q_sl, :])
            m = None
            for j in range(nk):
                sj = s[j * BLK:(j + 1) * BLK]
                if kind == FOX:
                    sj = sj - ck_ref[j]
                if j == qi:
                    sj = jnp.where(causal, sj, NEG_INF)
                s_ref[base + j] = sj
                cm = jnp.max(sj, axis=0, keepdims=True)
                if kind == MOBA and j < qi:
                    cm = cm + bias_ref[qi, j:j + 1, :]
                m = cm if m is None else jnp.maximum(m, cm)
            if kind == FOX:
                cq = cv[:, q_sl]
                shifts.append(cq - (m + cq))
            else:
                shifts.append(-m)
        return shifts

    def finish(qi, shifts):
        nk = qi + 1
        q_sl = slice(qi * BLK, (qi + 1) * BLK)
        v_all = vT_ref[:, 0:nk * BLK]
        outs = []
        for v in range(nvar):
            base = base_slot(qi, v)
            l = None
            for j in range(nk):
                sh = shifts[v] + bias_ref[qi, j:j + 1, :] if (kind == MOBA and j < qi) else shifts[v]
                pj = jnp.exp2(s_ref[base + j] + sh)
                lj = jnp.sum(pj, axis=0, keepdims=True)
                l = lj if l is None else l + lj
                p_ref[(base + j) * BLK:(base + j + 1) * BLK, :] = pj.astype(jnp.bfloat16)
            acc = _dot(v_all, p_ref[base * BLK:(base + nk) * BLK, :])
            outs.append(acc * (1.0 / l))
        if kind == DIFF:
            lam = (jnp.exp(jnp.sum(lam_ref[0:1, :] * lam_ref[1:2, :], axis=1, keepdims=True))
                   - jnp.exp(jnp.sum(lam_ref[2:3, :] * lam_ref[3:4, :], axis=1, keepdims=True))
                   + lambda_init)
            y = (outs[0] - lam * outs[1]).T
            msq = jnp.mean(y * y, axis=-1, keepdims=True)
            y = y * lax.rsqrt(msq + NORM_EPS) * sg_ref[...] * (1.0 - lambda_init)
        else:
            y = outs[0].T
        ov[q_sl, :] = y.astype(jnp.bfloat16)

    pending = {}
    lookahead = QK_LOOKAHEAD[kind]
    for step in range(nblk + lookahead):
        if step < nblk:
            pending[step] = scores(step)
        done = step - lookahead
        if done >= 0:
            finish(done, pending.pop(done))
    return carry


def _attention(kind, qs, k, v, extras, lambda_init=0.0):
    b, h, s, _ = k.shape
    nblk = s // BLK
    hg = ATTN_HEADS_PER_STEP
    nslots = len(qs) * nblk * (nblk + 1) // 2
    h_spec = pl.BlockSpec((1, hg, s, HEAD_DIM), lambda bi, hi: (bi, hi, 0, 0))
    in_specs = [h_spec] * (len(qs) + 2)
    scratch = [pltpu.VMEM((nslots, BLK, BLK), jnp.float32), pltpu.VMEM((nslots * BLK, BLK), jnp.bfloat16),
               pltpu.VMEM((HEAD_DIM, s), jnp.bfloat16)]
    if kind == MOBA:
        scratch.append(pltpu.VMEM((nblk, nblk, BLK), jnp.float32))
    elif kind == FOX:
        in_specs.append(pl.BlockSpec((1, hg, 1, s), lambda bi, hi: (bi, hi, 0, 0)))
        scratch.append(pltpu.VMEM((nblk, BLK, BLK), jnp.float32))
    else:
        in_specs += [pl.BlockSpec((4, DIFF_SUB_DIM), lambda bi, hi: (0, 0)),
                     pl.BlockSpec((1, HEAD_DIM), lambda bi, hi: (0, 0))]
    return pl.pallas_call(
        functools.partial(_attn_kernel, kind, nblk, hg, lambda_init),
        grid=(b, h // hg), in_specs=in_specs,
        out_specs=h_spec,
        out_shape=jax.ShapeDtypeStruct((b, h, s, HEAD_DIM), jnp.bfloat16),
        scratch_shapes=scratch,
        compiler_params=pltpu.CompilerParams(
            dimension_semantics=("arbitrary", "arbitrary"), vmem_limit_bytes=VMEM_LIMIT_BYTES),
        name=f"attn_{KIND_NAMES[kind]}",
    )(*qs, k, v, *extras)


def _gated(y_ref, sz_ref):
    y = jnp.concatenate([y_ref[0, h] for h in range(y_ref.shape[1])], axis=1)
    return (y.astype(jnp.float32) * sz_ref[0].astype(jnp.float32)).astype(jnp.bfloat16)


def _out_kernel(y_ref, sz_ref, x_ref, w_ref, o_ref):
    o_ref[0] = x_ref[0] + _dot(_gated(y_ref, sz_ref), w_ref[...])


def _out_project(y, sz, x, w, tm=1024):
    b, s, d = x.shape
    h = y.shape[1]
    width = h * HEAD_DIM
    row_spec = pl.BlockSpec((1, tm, d), lambda bi, si: (bi, si, 0))
    return pl.pallas_call(
        _out_kernel, grid=(b, s // tm),
        in_specs=[pl.BlockSpec((1, h, tm, HEAD_DIM), lambda bi, si: (bi, 0, si, 0)),
                  pl.BlockSpec((1, tm, width), lambda bi, si: (bi, si, 0)),
                  row_spec, pl.BlockSpec((width, d), lambda bi, si: (0, 0))],
        out_specs=row_spec,
        out_shape=jax.ShapeDtypeStruct((b, s, d), jnp.float32),
        compiler_params=pltpu.CompilerParams(
            dimension_semantics=("arbitrary", "arbitrary"), vmem_limit_bytes=VMEM_LIMIT_BYTES),
        name="out_proj",
    )(y, sz, x, w)


def _rope_tables(seq, dim):
    inv = 1.0 / (ROPE_THETA ** (jnp.arange(0, dim, 2, dtype=jnp.float32) / dim))
    ang = jnp.arange(seq, dtype=jnp.float32)[:, None] * inv[None, :]
    return jnp.cos(ang), jnp.sin(ang)


def kernel(x, norm_g, w_out, a_w_in, a_q_norm, a_k_norm, b_w_in, b_f_bias, b_q_norm, b_k_norm,
           c_w_in, c_q_norm, c_k_norm, c_lambda_q1, c_lambda_k1, c_lambda_q2, c_lambda_k2, c_subln):
    b, s, d = x.shape
    depth = norm_g.shape[0]
    width = N_HEADS * HEAD_DIM
    assert s % BLK == 0 and BLK == MOBA_BLOCK and d == width

    cos_h, sin_h = _rope_tables(s, HEAD_DIM)
    moba_tabs = (jnp.concatenate([cos_h, cos_h], axis=1), jnp.concatenate([-sin_h, sin_h], axis=1))
    cos_d, sin_d = _rope_tables(s, DIFF_SUB_DIM)
    zero = jnp.zeros_like(sin_d)
    diff_tabs = (jnp.concatenate([cos_d] * 4, axis=1),
                 jnp.concatenate([-sin_d, zero, -sin_d, zero], axis=1),
                 jnp.concatenate([zero, sin_d, zero, sin_d], axis=1))

    for i in range(depth):
        kind, j = i % 3, i // 3
        ng = norm_g[i][None, :]
        if kind == MOBA:
            w = a_w_in[j].astype(jnp.bfloat16)
            q, k, v, sz = _project(MOBA, x, ng, w, a_q_norm[j][None, :], a_k_norm[j][None, :], moba_tabs)
            y = _attention(MOBA, (q,), k, v, ())
        elif kind == FOX:
            w = jnp.pad(b_w_in[j], ((0, 0), (0, LANES - N_HEADS))).astype(jnp.bfloat16)
            fb = jnp.pad(b_f_bias[j], (0, LANES - N_HEADS))[None, :]
            q, k, v, sz, c = _project(FOX, x, ng, w, b_q_norm[j][None, :], b_k_norm[j][None, :], (fb,))
            y = _attention(FOX, (q,), k, v, (c,))
        else:
            w = c_w_in[j].astype(jnp.bfloat16)
            qg = jnp.concatenate([c_q_norm[j]] * 2)[None, :]
            kg = jnp.concatenate([c_k_norm[j]] * 2)[None, :]
            qa, qb, k, v, sz = _project(DIFF, x, ng, w, qg, kg, diff_tabs)
            lam = jnp.stack([c_lambda_q1[j], c_lambda_k1[j], c_lambda_q2[j], c_lambda_k2[j]])
            lambda_init = 0.8 - 0.6 * math.exp(-0.3 * i)
            y = _attention(DIFF, (qa, qb), k, v, (lam, c_subln[j][None, :]), lambda_init)
        x = _out_project(y, sz, x, w_out[i].astype(jnp.bfloat16))
    return x
```

```python
import functools
import math

import jax
import jax.numpy as jnp
from jax import lax
from jax.experimental import pallas as pl
from jax.experimental.pallas import tpu as pltpu

N_HEADS = 8
HEAD_DIM = 128
ROPE_THETA = 10000.0
NORM_EPS = 1e-6
MOBA_BLOCK = 256
MOBA_TOPK = 3
DIFF_SUB_DIM = HEAD_DIM // 2

BLK = 256
QK_LOOKAHEAD = (3, 3, 2)
ATTN_HEADS_PER_STEP = 4
LANES = 128
LOG2E = 1.4426950408889634
NEG_INF = float("-inf")
VMEM_LIMIT_BYTES = 56 * 1024 * 1024

MOBA, FOX, DIFF = 0, 1, 2
KIND_NAMES = ("moba", "fox", "diff")


def _dot(a, b):
    return jnp.dot(a, b, preferred_element_type=jnp.float32)


def _dot_nt(a, b):
    return lax.dot_general(a, b, (((1,), (1,)), ((), ())), preferred_element_type=jnp.float32)


def _head_norm_rope(kind, xh, ssq, gain, tabs, scale):
    norm_dim = DIFF_SUB_DIM if kind == DIFF else HEAD_DIM
    y = xh * lax.rsqrt(ssq * (1.0 / norm_dim) + NORM_EPS) * gain
    if kind == MOBA:
        cos_t, sin_t = tabs
        y = y * cos_t + pltpu.roll(y, HEAD_DIM // 2, axis=1) * sin_t
    elif kind == DIFF:
        cos_t, s_up, s_dn = tabs
        y = (y * cos_t + pltpu.roll(y, HEAD_DIM - DIFF_SUB_DIM // 2, axis=1) * s_up
             + pltpu.roll(y, DIFF_SUB_DIM // 2, axis=1) * s_dn)
    if scale != 1.0:
        y = y * scale
    return y


def _proj_kernel(kind, fused, tm, width, *refs):
    if fused:
        y_ref, szp_ref, wout_ref = refs[:3]
        refs = refs[3:]
    if kind == MOBA:
        (x_ref, ng_ref, w_ref, qg_ref, kg_ref, cos_ref, sin_ref, *outs, hdn_ref) = refs
        tabs = (cos_ref[...], sin_ref[...])
    elif kind == FOX:
        (x_ref, ng_ref, w_ref, qg_ref, kg_ref, fb_ref, *outs, hdn_ref, carry_ref) = refs
        tabs = None
    else:
        (x_ref, ng_ref, w_ref, qg_ref, kg_ref, cos_ref, sup_ref, sdn_ref, *outs, hdn_ref) = refs
        tabs = (cos_ref[...], sup_ref[...], sdn_ref[...])
    if fused:
        xnew_ref, *outs = outs
    if kind == MOBA:
        q_ref, k_ref, v_ref, sz_ref = outs
    elif kind == FOX:
        q_ref, k_ref, v_ref, sz_ref, c_ref = outs
    else:
        q_ref, qb_ref, k_ref, v_ref, sz_ref = outs

    if kind == FOX:
        @pl.when(pl.program_id(1) == 0)
        def _():
            carry_ref[...] = jnp.zeros_like(carry_ref)

    x = x_ref[0]
    if fused:
        x = x + _dot(_gated(y_ref, szp_ref), wout_ref[...])
        xnew_ref[0] = x
    ms = jnp.mean(x * x, axis=-1, keepdims=True)
    hdn_ref[...] = (x * lax.rsqrt(ms + NORM_EPS) * ng_ref[...]).astype(jnp.bfloat16)

    norm_dim = DIFF_SUB_DIM if kind == DIFF else HEAD_DIM
    q_scale = (norm_dim ** -0.5) * LOG2E
    pair = 2 * HEAD_DIM
    shift = norm_dim.bit_length() - 1
    gi = lax.broadcasted_iota(jnp.int32, (pair, pair), 0) >> shift
    gj = lax.broadcasted_iota(jnp.int32, (pair, pair), 1) >> shift
    group_ones = jnp.where(gi == gj, 1.0, 0.0).astype(jnp.bfloat16)

    def head_dots(p):
        return tuple(_dot(hdn_ref[...], w_ref[:, g * width + p * pair:g * width + (p + 1) * pair])
                     for g in range(3))

    def head_epilogue(p, rq, rk, rv):
        ssq_q = _dot((rq * rq).astype(jnp.bfloat16), group_ones)
        ssq_k = _dot((rk * rk).astype(jnp.bfloat16), group_ones)
        for half in range(2):
            h = 2 * p + half
            sl = slice(half * HEAD_DIM, (half + 1) * HEAD_DIM)
            q = _head_norm_rope(kind, rq[:, sl], ssq_q[:, sl], qg_ref[...], tabs, q_scale)
            kk = _head_norm_rope(kind, rk[:, sl], ssq_k[:, sl], kg_ref[...], tabs, 1.0)
            if kind == DIFF:
                lane = lax.broadcasted_iota(jnp.int32, q.shape, 1)
                lo = lane < DIFF_SUB_DIM
                q_ref[0, h] = jnp.where(lo, q, 0.0).astype(jnp.bfloat16)
                qb_ref[0, h] = jnp.where(lo, 0.0, q).astype(jnp.bfloat16)
            else:
                q_ref[0, h] = q.astype(jnp.bfloat16)
            k_ref[0, h] = kk.astype(jnp.bfloat16)
            v_ref[0, h] = rv[:, sl].astype(jnp.bfloat16)

    def forget_gates(fl):
        logf = jnp.minimum(fl, 0.0) - jnp.log(1.0 + jnp.exp(-jnp.abs(fl)))
        row = lax.broadcasted_iota(jnp.int32, (tm, tm), 0)
        col = lax.broadcasted_iota(jnp.int32, (tm, tm), 1)
        tril = jnp.where(row >= col, 1.0, 0.0).astype(jnp.bfloat16)
        hi = logf.astype(jnp.bfloat16)
        r1 = logf - hi.astype(jnp.float32)
        mid = r1.astype(jnp.bfloat16)
        lo_ = (r1 - mid.astype(jnp.float32)).astype(jnp.bfloat16)
        csum = _dot(tril, hi) + _dot(tril, mid) + _dot(tril, lo_)
        c_tile = csum + carry_ref[0:1, :]
        carry_ref[0:1, :] = c_tile[tm - 1:tm, :]
        cT = (c_tile * LOG2E).T
        for h in range(N_HEADS):
            c_ref[0, h] = cT[h:h + 1, :]

    if kind == FOX:
        fl = _dot(hdn_ref[...], w_ref[:, 4 * width:4 * width + LANES]) + fb_ref[...]
    prev = None
    for p in range(N_HEADS // 2):
        cur = head_dots(p)
        if kind == FOX and p == 0:
            forget_gates(fl)
        if prev is not None:
            head_epilogue(p - 1, *prev)
        prev = cur
    zc = width // 2
    rz = [None, None]
    rz[0] = _dot(hdn_ref[...], w_ref[:, 3 * width:3 * width + zc])
    head_epilogue(N_HEADS // 2 - 1, *prev)
    rz[1] = _dot(hdn_ref[...], w_ref[:, 3 * width + zc:4 * width])
    for c in range(2):
        sz_ref[0, :, c * zc:(c + 1) * zc] = (rz[c] * (1.0 / (1.0 + jnp.exp(-rz[c])))).astype(jnp.bfloat16)


def _project(kind, x, ng, w, qg, kg, extras, prev=None, tm=512):
    b, s, d = x.shape
    width = N_HEADS * HEAD_DIM
    grid = (b, s // tm)
    fused = prev is not None

    def const(shape):
        return pl.BlockSpec(shape, lambda bi, si: (0,) * len(shape))

    tab_spec = pl.BlockSpec((tm, HEAD_DIM), lambda bi, si: (si, 0))
    x_spec = pl.BlockSpec((1, tm, d), lambda bi, si: (bi, si, 0))
    h_spec = pl.BlockSpec((1, N_HEADS, tm, HEAD_DIM), lambda bi, si: (bi, 0, si, 0))
    sz_spec = pl.BlockSpec((1, tm, width), lambda bi, si: (bi, si, 0))
    in_specs = [h_spec, sz_spec, const((width, d))] if fused else []
    in_specs += [x_spec, const((1, d)), const(w.shape), const((1, HEAD_DIM)), const((1, HEAD_DIM))]
    if kind == MOBA:
        in_specs += [tab_spec, tab_spec]
    elif kind == FOX:
        in_specs += [const((1, LANES))]
    else:
        in_specs += [tab_spec, tab_spec, tab_spec]

    h_shape = jax.ShapeDtypeStruct((b, N_HEADS, s, HEAD_DIM), jnp.bfloat16)
    sz_shape = jax.ShapeDtypeStruct((b, s, width), jnp.bfloat16)

    n_heads_out = 4 if kind == DIFF else 3
    out_shape = [h_shape] * n_heads_out + [sz_shape]
    out_specs = [h_spec] * n_heads_out + [sz_spec]
    if fused:
        out_shape.insert(0, jax.ShapeDtypeStruct((b, s, d), jnp.float32))
        out_specs.insert(0, x_spec)
    scratch = [pltpu.VMEM((tm, d), jnp.bfloat16)]
    if kind == FOX:
        out_shape.append(jax.ShapeDtypeStruct((b, N_HEADS, 1, s), jnp.float32))
        out_specs.append(pl.BlockSpec((1, N_HEADS, 1, tm), lambda bi, si: (bi, 0, 0, si)))
        scratch.append(pltpu.VMEM((8, LANES), jnp.float32))

    return pl.pallas_call(
        functools.partial(_proj_kernel, kind, fused, tm, width),
        grid=grid, in_specs=in_specs, out_specs=out_specs, out_shape=out_shape,
        scratch_shapes=scratch,
        compiler_params=pltpu.CompilerParams(
            dimension_semantics=("arbitrary", "arbitrary"), vmem_limit_bytes=VMEM_LIMIT_BYTES),
        name=f"proj_{KIND_NAMES[kind]}",
    )(*(prev or ()), x, ng, w, qg, kg, *extras)


def _attn_kernel(kind, nblk, heads, lambda_init, *refs):
    if kind == MOBA:
        q_ref, k_ref, v_ref, o_ref, s_ref, p_ref, vT_ref, bias_ref = refs
        q_refs = (q_ref,)
    elif kind == FOX:
        q_ref, k_ref, v_ref, c_ref, o_ref, s_ref, p_ref, vT_ref, ck_ref = refs
        q_refs = (q_ref,)
    else:
        q_ref, qb_ref, k_ref, v_ref, lam_ref, sg_ref, o_ref, s_ref, p_ref, vT_ref = refs
        q_refs = (q_ref, qb_ref)
    nvar = len(q_refs)
    lax.fori_loop(0, heads, functools.partial(
        _attn_head, kind, nblk, lambda_init, refs, q_refs), 0)


def _attn_head(kind, nblk, lambda_init, refs, q_refs, hh, carry):
    if kind == MOBA:
        _, k_ref, v_ref, o_ref, s_ref, p_ref, vT_ref, bias_ref = refs
    elif kind == FOX:
        _, k_ref, v_ref, c_ref, o_ref, s_ref, p_ref, vT_ref, ck_ref = refs
        cv = c_ref.at[0, hh]
    else:
        _, _, k_ref, v_ref, lam_ref, sg_ref, o_ref, s_ref, p_ref, vT_ref = refs
    nvar = len(q_refs)
    qv = [r.at[0, hh] for r in q_refs]
    kv = k_ref.at[0, hh]
    ov = o_ref.at[0, hh]

    vT_ref[...] = v_ref[0, hh].T

    if kind == MOBA:
        kmean = jnp.concatenate(
            [jnp.sum(kv[j * BLK:(j + 1) * BLK, :].astype(jnp.float32), axis=0, keepdims=True)
             for j in range(nblk)], axis=0)
        kmean = (kmean * (1.0 / MOBA_BLOCK)).astype(jnp.bfloat16)
        row = lax.broadcasted_iota(jnp.int32, (nblk, BLK), 0)
        for t in range(1, nblk):
            gate = _dot_nt(kmean, qv[0][t * BLK:(t + 1) * BLK, :])
            past = row < t
            gm = jnp.where(past, gate, NEG_INF)
            cnt = jnp.zeros((nblk, BLK), jnp.int32)
            for r in range(1, nblk):
                other = pltpu.roll(gm, r, axis=0)
                tie_first = jnp.where(row >= r, 1, 0)
                cnt = cnt + jnp.where(other > gm, 1, jnp.where(other == gm, tie_first, 0))
            bias_ref[t] = jnp.where(past, jnp.where(cnt < MOBA_TOPK, 0.0, NEG_INF), NEG_INF)
    elif kind == FOX:
        for j in range(nblk):
            ck_ref[j] = jnp.broadcast_to(cv[:, j * BLK:(j + 1) * BLK], (BLK, BLK)).T

    krow = lax.broadcasted_iota(jnp.int32, (BLK, BLK), 0)
    qcol = lax.broadcasted_iota(jnp.int32, (BLK, BLK), 1)
    causal = krow <= qcol

    def base_slot(qi, v):
        return nvar * qi * (qi + 1) // 2 + v * (qi + 1)

    def scores(qi):
        nk = qi + 1
        q_sl = slice(qi * BLK, (qi + 1) * BLK)
        k_all = kv[0:nk * BLK, :]
        shifts = []
        for v in range(nvar):
            base = base_slot(qi, v)
            s = _dot_nt(k_all, qv[v][q_sl, :])
            m = None
            for j in range(nk):
                sj = s[j * BLK:(j + 1) * BLK]
                if kind == FOX:
                    sj = sj - ck_ref[j]
                if j == qi:
                    sj = jnp.where(causal, sj, NEG_INF)
                s_ref[base + j] = sj
                cm = jnp.max(sj, axis=0, keepdims=True)
                if kind == MOBA and j < qi:
                    cm = cm + bias_ref[qi, j:j + 1, :]
                m = cm if m is None else jnp.maximum(m, cm)
            if kind == FOX:
                cq = cv[:, q_sl]
                shifts.append(cq - (m + cq))
            else:
                shifts.append(-m)
        return shifts

    def finish(qi, shifts):
        nk = qi + 1
        q_sl = slice(qi * BLK, (qi + 1) * BLK)
        v_all = vT_ref[:, 0:nk * BLK]
        outs = []
        for v in range(nvar):
            base = base_slot(qi, v)
            l = None
            for j in range(nk):
                sh = shifts[v] + bias_ref[qi, j:j + 1, :] if (kind == MOBA and j < qi) else shifts[v]
                pj = jnp.exp2(s_ref[base + j] + sh)
                lj = jnp.sum(pj, axis=0, keepdims=True)
                l = lj if l is None else l + lj
                p_ref[(base + j) * BLK:(base + j + 1) * BLK, :] = pj.astype(jnp.bfloat16)
            acc = _dot(v_all, p_ref[base * BLK:(base + nk) * BLK, :])
            outs.append(acc * (1.0 / l))
        if kind == DIFF:
            lam = (jnp.exp(jnp.sum(lam_ref[0:1, :] * lam_ref[1:2, :], axis=1, keepdims=True))
                   - jnp.exp(jnp.sum(lam_ref[2:3, :] * lam_ref[3:4, :], axis=1, keepdims=True))
                   + lambda_init)
            y = (outs[0] - lam * outs[1]).T
            msq = jnp.mean(y * y, axis=-1, keepdims=True)
            y = y * lax.rsqrt(msq + NORM_EPS) * sg_ref[...] * (1.0 - lambda_init)
        else:
            y = outs[0].T
        ov[q_sl, :] = y.astype(jnp.bfloat16)

    pending = {}
    lookahead = QK_LOOKAHEAD[kind]
    order = list(range(nblk - 1, -1, -1))
    for step in range(nblk + lookahead):
        if step < nblk:
            pending[order[step]] = scores(order[step])
        done = step - lookahead
        if done >= 0:
            finish(order[done], pending.pop(order[done]))
    return carry


def _attention(kind, qs, k, v, extras, lambda_init=0.0):
    b, h, s, _ = k.shape
    nblk = s // BLK
    hg = ATTN_HEADS_PER_STEP
    nslots = len(qs) * nblk * (nblk + 1) // 2
    h_spec = pl.BlockSpec((1, hg, s, HEAD_DIM), lambda bi, hi: (bi, hi, 0, 0))
    in_specs = [h_spec] * (len(qs) + 2)
    scratch = [pltpu.VMEM((nslots, BLK, BLK), jnp.float32), pltpu.VMEM((nslots * BLK, BLK), jnp.bfloat16),
               pltpu.VMEM((HEAD_DIM, s), jnp.bfloat16)]
    if kind == MOBA:
        scratch.append(pltpu.VMEM((nblk, nblk, BLK), jnp.float32))
    elif kind == FOX:
        in_specs.append(pl.BlockSpec((1, hg, 1, s), lambda bi, hi: (bi, hi, 0, 0)))
        scratch.append(pltpu.VMEM((nblk, BLK, BLK), jnp.float32))
    else:
        in_specs += [pl.BlockSpec((4, DIFF_SUB_DIM), lambda bi, hi: (0, 0)),
                     pl.BlockSpec((1, HEAD_DIM), lambda bi, hi: (0, 0))]
    return pl.pallas_call(
        functools.partial(_attn_kernel, kind, nblk, hg, lambda_init),
        grid=(b, h // hg), in_specs=in_specs,
        out_specs=h_spec,
        out_shape=jax.ShapeDtypeStruct((b, h, s, HEAD_DIM), jnp.bfloat16),
        scratch_shapes=scratch,
        compiler_params=pltpu.CompilerParams(
            dimension_semantics=("arbitrary", "arbitrary"), vmem_limit_bytes=VMEM_LIMIT_BYTES),
        name=f"attn_{KIND_NAMES[kind]}",
    )(*qs, k, v, *extras)


def _gated(y_ref, sz_ref):
    y = jnp.concatenate([y_ref[0, h] for h in range(y_ref.shape[1])], axis=1)
    return (y.astype(jnp.float32) * sz_ref[0].astype(jnp.float32)).astype(jnp.bfloat16)


def _out_kernel(y_ref, sz_ref, x_ref, w_ref, o_ref):
    o_ref[0] = x_ref[0] + _dot(_gated(y_ref, sz_ref), w_ref[...])


def _out_project(y, sz, x, w, tm=1024):
    b, s, d = x.shape
    h = y.shape[1]
    width = h * HEAD_DIM
    row_spec = pl.BlockSpec((1, tm, d), lambda bi, si: (bi, si, 0))
    return pl.pallas_call(
        _out_kernel, grid=(b, s // tm),
        in_specs=[pl.BlockSpec((1, h, tm, HEAD_DIM), lambda bi, si: (bi, 0, si, 0)),
                  pl.BlockSpec((1, tm, width), lambda bi, si: (bi, si, 0)),
                  row_spec, pl.BlockSpec((width, d), lambda bi, si: (0, 0))],
        out_specs=row_spec,
        out_shape=jax.ShapeDtypeStruct((b, s, d), jnp.float32),
        compiler_params=pltpu.CompilerParams(
            dimension_semantics=("arbitrary", "arbitrary"), vmem_limit_bytes=VMEM_LIMIT_BYTES),
        name="out_proj",
    )(y, sz, x, w)


def _rope_tables(seq, dim):
    inv = 1.0 / (ROPE_THETA ** (jnp.arange(0, dim, 2, dtype=jnp.float32) / dim))
    ang = jnp.arange(seq, dtype=jnp.float32)[:, None] * inv[None, :]
    return jnp.cos(ang), jnp.sin(ang)


def kernel(x, norm_g, w_out, a_w_in, a_q_norm, a_k_norm, b_w_in, b_f_bias, b_q_norm, b_k_norm,
           c_w_in, c_q_norm, c_k_norm, c_lambda_q1, c_lambda_k1, c_lambda_q2, c_lambda_k2, c_subln):
    b, s, d = x.shape
    depth = norm_g.shape[0]
    width = N_HEADS * HEAD_DIM
    assert s % BLK == 0 and BLK == MOBA_BLOCK and d == width

    cos_h, sin_h = _rope_tables(s, HEAD_DIM)
    moba_tabs = (jnp.concatenate([cos_h, cos_h], axis=1), jnp.concatenate([-sin_h, sin_h], axis=1))
    cos_d, sin_d = _rope_tables(s, DIFF_SUB_DIM)
    zero = jnp.zeros_like(sin_d)
    diff_tabs = (jnp.concatenate([cos_d] * 4, axis=1),
                 jnp.concatenate([-sin_d, zero, -sin_d, zero], axis=1),
                 jnp.concatenate([zero, sin_d, zero, sin_d], axis=1))

    prev = None
    for i in range(depth):
        kind, j = i % 3, i // 3
        ng = norm_g[i][None, :]
        if kind == MOBA:
            w = a_w_in[j].astype(jnp.bfloat16)
            outs = _project(MOBA, x, ng, w, a_q_norm[j][None, :], a_k_norm[j][None, :], moba_tabs, prev)
        elif kind == FOX:
            w = jnp.pad(b_w_in[j], ((0, 0), (0, LANES - N_HEADS))).astype(jnp.bfloat16)
            fb = jnp.pad(b_f_bias[j], (0, LANES - N_HEADS))[None, :]
            outs = _project(FOX, x, ng, w, b_q_norm[j][None, :], b_k_norm[j][None, :], (fb,), prev)
        else:
            w = c_w_in[j].astype(jnp.bfloat16)
            qg = jnp.concatenate([c_q_norm[j]] * 2)[None, :]
            kg = jnp.concatenate([c_k_norm[j]] * 2)[None, :]
            outs = _project(DIFF, x, ng, w, qg, kg, diff_tabs, prev)
        if prev is not None:
            x, *outs = outs
        if kind == MOBA:
            q, k, v, sz = outs
            y = _attention(MOBA, (q,), k, v, ())
        elif kind == FOX:
            q, k, v, sz, c = outs
            y = _attention(FOX, (q,), k, v, (c,))
        else:
            qa, qb, k, v, sz = outs
            lam = jnp.stack([c_lambda_q1[j], c_lambda_k1[j], c_lambda_q2[j], c_lambda_k2[j]])
            lambda_init = 0.8 - 0.6 * math.exp(-0.3 * i)
            y = _attention(DIFF, (qa, qb), k, v, (lam, c_subln[j][None, :]), lambda_init)
        prev = (y, sz, w_out[i].astype(jnp.bfloat16))
    return _out_project(*prev[:2], x, prev[2])
```

```python
import functools
import math

import jax
import jax.numpy as jnp
from jax import lax
from jax.experimental import pallas as pl
from jax.experimental.pallas import tpu as pltpu

N_HEADS = 8
HEAD_DIM = 128
ROPE_THETA = 10000.0
NORM_EPS = 1e-6
MOBA_BLOCK = 256
MOBA_TOPK = 3
DIFF_SUB_DIM = HEAD_DIM // 2

BLK = 256
QK_LOOKAHEAD = (3, 3, 2)
ATTN_HEADS_PER_STEP = 4
HEADS_INTERLEAVED = (2, 2, 1)
TILE_ORDER = (lambda n: list(range(n)), lambda n: list(range(n)), lambda n: list(range(n - 1, -1, -1)))
LANES = 128
LOG2E = 1.4426950408889634
NEG_INF = float("-inf")
VMEM_LIMIT_BYTES = 56 * 1024 * 1024

MOBA, FOX, DIFF = 0, 1, 2
KIND_NAMES = ("moba", "fox", "diff")


def _dot(a, b):
    return jnp.dot(a, b, preferred_element_type=jnp.float32)


def _dot_nt(a, b):
    return lax.dot_general(a, b, (((1,), (1,)), ((), ())), preferred_element_type=jnp.float32)


def _head_norm_rope(kind, xh, ssq, gain, tabs, scale):
    norm_dim = DIFF_SUB_DIM if kind == DIFF else HEAD_DIM
    y = xh * lax.rsqrt(ssq * (1.0 / norm_dim) + NORM_EPS) * gain
    if kind == MOBA:
        cos_t, sin_t = tabs
        y = y * cos_t + pltpu.roll(y, HEAD_DIM // 2, axis=1) * sin_t
    elif kind == DIFF:
        cos_t, s_up, s_dn = tabs
        y = (y * cos_t + pltpu.roll(y, HEAD_DIM - DIFF_SUB_DIM // 2, axis=1) * s_up
             + pltpu.roll(y, DIFF_SUB_DIM // 2, axis=1) * s_dn)
    if scale != 1.0:
        y = y * scale
    return y


def _proj_kernel(kind, fused, tm, width, *refs):
    if fused:
        y_ref, szp_ref, wout_ref = refs[:3]
        refs = refs[3:]
    if kind == MOBA:
        (x_ref, ng_ref, w_ref, qg_ref, kg_ref, cos_ref, sin_ref, *outs, hdn_ref) = refs
        tabs = (cos_ref[...], sin_ref[...])
    elif kind == FOX:
        (x_ref, ng_ref, w_ref, qg_ref, kg_ref, fb_ref, *outs, hdn_ref, carry_ref) = refs
        tabs = None
    else:
        (x_ref, ng_ref, w_ref, qg_ref, kg_ref, cos_ref, sup_ref, sdn_ref, *outs, hdn_ref) = refs
        tabs = (cos_ref[...], sup_ref[...], sdn_ref[...])
    if fused:
        xnew_ref, *outs = outs
    if kind == MOBA:
        q_ref, k_ref, v_ref, sz_ref = outs
    elif kind == FOX:
        q_ref, k_ref, v_ref, sz_ref, c_ref = outs
    else:
        q_ref, qb_ref, k_ref, v_ref, sz_ref = outs

    if kind == FOX:
        @pl.when(pl.program_id(1) == 0)
        def _():
            carry_ref[...] = jnp.zeros_like(carry_ref)

    x = x_ref[0]
    if fused:
        x = x + _dot(_gated(y_ref, szp_ref), wout_ref[...])
        xnew_ref[0] = x
    ms = jnp.mean(x * x, axis=-1, keepdims=True)
    hdn_ref[...] = (x * lax.rsqrt(ms + NORM_EPS) * ng_ref[...]).astype(jnp.bfloat16)

    norm_dim = DIFF_SUB_DIM if kind == DIFF else HEAD_DIM
    q_scale = (norm_dim ** -0.5) * LOG2E
    pair = 2 * HEAD_DIM
    shift = norm_dim.bit_length() - 1
    gi = lax.broadcasted_iota(jnp.int32, (pair, pair), 0) >> shift
    gj = lax.broadcasted_iota(jnp.int32, (pair, pair), 1) >> shift
    group_ones = jnp.where(gi == gj, 1.0, 0.0).astype(jnp.bfloat16)

    def head_dots(p):
        return tuple(_dot(hdn_ref[...], w_ref[:, g * width + p * pair:g * width + (p + 1) * pair])
                     for g in range(3))

    def head_epilogue(p, rq, rk, rv):
        if kind != FOX:
            ssq_q = _dot((rq * rq).astype(jnp.bfloat16), group_ones)
            ssq_k = _dot((rk * rk).astype(jnp.bfloat16), group_ones)
        for half in range(2):
            h = 2 * p + half
            sl = slice(half * HEAD_DIM, (half + 1) * HEAD_DIM)
            if kind == FOX:
                sq, sk = (jnp.sum(r[:, sl] * r[:, sl], axis=-1, keepdims=True) for r in (rq, rk))
            else:
                sq, sk = ssq_q[:, sl], ssq_k[:, sl]
            q = _head_norm_rope(kind, rq[:, sl], sq, qg_ref[...], tabs, q_scale)
            kk = _head_norm_rope(kind, rk[:, sl], sk, kg_ref[...], tabs, 1.0)
            if kind == DIFF:
                lane = lax.broadcasted_iota(jnp.int32, q.shape, 1)
                lo = lane < DIFF_SUB_DIM
                q_ref[0, h] = jnp.where(lo, q, 0.0).astype(jnp.bfloat16)
                qb_ref[0, h] = jnp.where(lo, 0.0, q).astype(jnp.bfloat16)
            else:
                q_ref[0, h] = q.astype(jnp.bfloat16)
            k_ref[0, h] = kk.astype(jnp.bfloat16)
            v_ref[0, h] = rv[:, sl].astype(jnp.bfloat16)

    def forget_gates(fl):
        head_lanes = lax.broadcasted_iota(jnp.int32, fl.shape, 1) < N_HEADS
        logf = jnp.minimum(fl, 0.0) - jnp.log(1.0 + jnp.exp(-jnp.abs(fl)))
        logf = jnp.where(head_lanes, logf, 0.0)
        row = lax.broadcasted_iota(jnp.int32, (tm, tm), 0)
        col = lax.broadcasted_iota(jnp.int32, (tm, tm), 1)
        tril = jnp.where(row >= col, 1.0, 0.0).astype(jnp.bfloat16)
        hi = logf.astype(jnp.bfloat16).astype(jnp.float32)
        r1 = logf - hi
        mid = r1.astype(jnp.bfloat16).astype(jnp.float32)
        lo_ = (r1 - mid).astype(jnp.bfloat16).astype(jnp.float32)
        parts = hi + pltpu.roll(mid, N_HEADS, axis=1) + pltpu.roll(lo_, 2 * N_HEADS, axis=1)
        cs = _dot(tril, parts.astype(jnp.bfloat16))
        csum = cs + pltpu.roll(cs, LANES - N_HEADS, axis=1) + pltpu.roll(cs, LANES - 2 * N_HEADS, axis=1)
        csum = jnp.where(head_lanes, csum, 0.0)
        c_tile = csum + carry_ref[0:1, :]
        carry_ref[0:1, :] = c_tile[tm - 1:tm, :]
        cT = (c_tile * LOG2E).T
        for h in range(N_HEADS):
            c_ref[0, h] = cT[h:h + 1, :]

    if kind == FOX:
        fl = _dot(hdn_ref[...], w_ref[:, 4 * width:4 * width + LANES]) + fb_ref[...]
    prev = None
    for p in range(N_HEADS // 2):
        cur = head_dots(p)
        if kind == FOX and p == 0:
            forget_gates(fl)
        if prev is not None:
            head_epilogue(p - 1, *prev)
        prev = cur
    zc = width // 2
    rz = [None, None]
    rz[0] = _dot(hdn_ref[...], w_ref[:, 3 * width:3 * width + zc])
    head_epilogue(N_HEADS // 2 - 1, *prev)
    rz[1] = _dot(hdn_ref[...], w_ref[:, 3 * width + zc:4 * width])
    for c in range(2):
        sz_ref[0, :, c * zc:(c + 1) * zc] = (rz[c] * (1.0 / (1.0 + jnp.exp(-rz[c])))).astype(jnp.bfloat16)


def _project(kind, x, ng, w, qg, kg, extras, prev=None, tm=512):
    b, s, d = x.shape
    width = N_HEADS * HEAD_DIM
    grid = (b, s // tm)
    fused = prev is not None

    def const(shape):
        return pl.BlockSpec(shape, lambda bi, si: (0,) * len(shape))

    tab_spec = pl.BlockSpec((tm, HEAD_DIM), lambda bi, si: (si, 0))
    x_spec = pl.BlockSpec((1, tm, d), lambda bi, si: (bi, si, 0))
    h_spec = pl.BlockSpec((1, N_HEADS, tm, HEAD_DIM), lambda bi, si: (bi, 0, si, 0))
    sz_spec = pl.BlockSpec((1, tm, width), lambda bi, si: (bi, si, 0))
    in_specs = [h_spec, sz_spec, const((width, d))] if fused else []
    in_specs += [x_spec, const((1, d)), const(w.shape), const((1, HEAD_DIM)), const((1, HEAD_DIM))]
    if kind == MOBA:
        in_specs += [tab_spec, tab_spec]
    elif kind == FOX:
        in_specs += [const((1, LANES))]
    else:
        in_specs += [tab_spec, tab_spec, tab_spec]

    h_shape = jax.ShapeDtypeStruct((b, N_HEADS, s, HEAD_DIM), jnp.bfloat16)
    sz_shape = jax.ShapeDtypeStruct((b, s, width), jnp.bfloat16)

    n_heads_out = 4 if kind == DIFF else 3
    out_shape = [h_shape] * n_heads_out + [sz_shape]
    out_specs = [h_spec] * n_heads_out + [sz_spec]
    if fused:
        out_shape.insert(0, jax.ShapeDtypeStruct((b, s, d), jnp.float32))
        out_specs.insert(0, x_spec)
    scratch = [pltpu.VMEM((tm, d), jnp.bfloat16)]
    if kind == FOX:
        out_shape.append(jax.ShapeDtypeStruct((b, N_HEADS, 1, s), jnp.float32))
        out_specs.append(pl.BlockSpec((1, N_HEADS, 1, tm), lambda bi, si: (bi, 0, 0, si)))
        scratch.append(pltpu.VMEM((8, LANES), jnp.float32))

    return pl.pallas_call(
        functools.partial(_proj_kernel, kind, fused, tm, width),
        grid=grid, in_specs=in_specs, out_specs=out_specs, out_shape=out_shape,
        scratch_shapes=scratch,
        compiler_params=pltpu.CompilerParams(
            dimension_semantics=("arbitrary", "arbitrary"), vmem_limit_bytes=VMEM_LIMIT_BYTES),
        name=f"proj_{KIND_NAMES[kind]}",
    )(*(prev or ()), x, ng, w, qg, kg, *extras)


def _attn_kernel(kind, nblk, heads, lambda_init, *refs):
    if kind == MOBA:
        q_ref, k_ref, v_ref, o_ref, s_ref, p_ref, vT_ref, bias_ref = refs
        q_refs = (q_ref,)
    elif kind == FOX:
        q_ref, k_ref, v_ref, c_ref, o_ref, s_ref, p_ref, vT_ref, ck_ref = refs
        q_refs = (q_ref,)
    else:
        q_ref, qb_ref, k_ref, v_ref, lam_ref, sg_ref, o_ref, s_ref, p_ref, vT_ref = refs
        q_refs = (q_ref, qb_ref)
    ways = HEADS_INTERLEAVED[kind]

    def group(g, carry):
        stages = [_attn_head_stages(kind, nblk, lambda_init, refs, q_refs, w, g * ways + w) for w in range(ways)]
        pending = [{} for _ in range(ways)]
        lookahead = QK_LOOKAHEAD[kind]
        order = TILE_ORDER[kind](nblk)
        for step in range(nblk + lookahead):
            for w, (scores, finish) in enumerate(stages):
                if step < nblk:
                    pending[w][order[step]] = scores(order[step])
            done = step - lookahead
            for w, (scores, finish) in enumerate(stages):
                if done >= 0:
                    finish(order[done], pending[w].pop(order[done]))
        return carry

    lax.fori_loop(0, heads // ways, group, 0)


def _attn_head_stages(kind, nblk, lambda_init, refs, q_refs, w, hh):
    if kind == MOBA:
        _, k_ref, v_ref, o_ref, s_ref, p_ref, vT_ref, bias_ref = refs
        bias_ref = bias_ref.at[w]
    elif kind == FOX:
        _, k_ref, v_ref, c_ref, o_ref, s_ref, p_ref, vT_ref, ck_ref = refs
        ck_ref = ck_ref.at[w]
        cv = c_ref.at[0, hh]
    else:
        _, _, k_ref, v_ref, lam_ref, sg_ref, o_ref, s_ref, p_ref, vT_ref = refs
    s_ref, p_ref, vT_ref = s_ref.at[w], p_ref.at[w], vT_ref.at[w]
    nvar = len(q_refs)
    qv = [r.at[0, hh] for r in q_refs]
    kv = k_ref.at[0, hh]
    ov = o_ref.at[0, hh]

    vT_ref[...] = v_ref[0, hh].T

    if kind == MOBA:
        kmean = jnp.concatenate(
            [jnp.sum(kv[j * BLK:(j + 1) * BLK, :].astype(jnp.float32), axis=0, keepdims=True)
             for j in range(nblk)], axis=0)
        kmean = (kmean * (1.0 / MOBA_BLOCK)).astype(jnp.bfloat16)
        row = lax.broadcasted_iota(jnp.int32, (nblk, BLK), 0)
        for t in range(1, nblk):
            gate = _dot_nt(kmean, qv[0][t * BLK:(t + 1) * BLK, :])
            past = row < t
            gm = jnp.where(past, gate, NEG_INF)
            cnt = jnp.zeros((nblk, BLK), jnp.int32)
            for r in range(1, nblk):
                other = pltpu.roll(gm, r, axis=0)
                tie_first = jnp.where(row >= r, 1, 0)
                cnt = cnt + jnp.where(other > gm, 1, jnp.where(other == gm, tie_first, 0))
            bias_ref[t] = jnp.where(past, jnp.where(cnt < MOBA_TOPK, 0.0, NEG_INF), NEG_INF)
    elif kind == FOX:
        for j in range(nblk):
            ck_ref[j] = jnp.broadcast_to(cv[:, j * BLK:(j + 1) * BLK], (BLK, BLK)).T

    krow = lax.broadcasted_iota(jnp.int32, (BLK, BLK), 0)
    qcol = lax.broadcasted_iota(jnp.int32, (BLK, BLK), 1)
    causal = krow <= qcol

    def base_slot(qi, v):
        return nvar * qi * (qi + 1) // 2 + v * (qi + 1)

    def scores(qi):
        nk = qi + 1
        q_sl = slice(qi * BLK, (qi + 1) * BLK)
        k_all = kv[0:nk * BLK, :]
        shifts = []
        for v in range(nvar):
            base = base_slot(qi, v)
            s = _dot_nt(k_all, qv[v][q_sl, :])
            m = None
            for j in range(nk):
                sj = s[j * BLK:(j + 1) * BLK]
                if kind == FOX:
                    sj = sj - ck_ref[j]
                if j == qi:
                    sj = jnp.where(causal, sj, NEG_INF)
                s_ref[base + j] = sj
                cm = jnp.max(sj, axis=0, keepdims=True)
                if kind == MOBA and j < qi:
                    cm = cm + bias_ref[qi, j:j + 1, :]
                m = cm if m is None else jnp.maximum(m, cm)
            if kind == FOX:
                cq = cv[:, q_sl]
                shifts.append(cq - (m + cq))
            else:
                shifts.append(-m)
        return shifts

    def finish(qi, shifts):
        nk = qi + 1
        q_sl = slice(qi * BLK, (qi + 1) * BLK)
        v_all = vT_ref[:, 0:nk * BLK]
        outs = []
        for v in range(nvar):
            base = base_slot(qi, v)
            l = None
            for j in range(nk):
                sh = shifts[v] + bias_ref[qi, j:j + 1, :] if (kind == MOBA and j < qi) else shifts[v]
                pj = jnp.exp2(s_ref[base + j] + sh)
                lj = jnp.sum(pj, axis=0, keepdims=True)
                l = lj if l is None else l + lj
                p_ref[(base + j) * BLK:(base + j + 1) * BLK, :] = pj.astype(jnp.bfloat16)
            acc = _dot(v_all, p_ref[base * BLK:(base + nk) * BLK, :])
            outs.append(acc * (1.0 / l))
        if kind == DIFF:
            lam = (jnp.exp(jnp.sum(lam_ref[0:1, :] * lam_ref[1:2, :], axis=1, keepdims=True))
                   - jnp.exp(jnp.sum(lam_ref[2:3, :] * lam_ref[3:4, :], axis=1, keepdims=True))
                   + lambda_init)
            y = (outs[0] - lam * outs[1]).T
            msq = jnp.mean(y * y, axis=-1, keepdims=True)
            y = y * lax.rsqrt(msq + NORM_EPS) * sg_ref[...] * (1.0 - lambda_init)
        else:
            y = outs[0].T
        ov[q_sl, :] = y.astype(jnp.bfloat16)

    return scores, finish


def _attention(kind, qs, k, v, extras, lambda_init=0.0):
    b, h, s, _ = k.shape
    nblk = s // BLK
    hg = ATTN_HEADS_PER_STEP
    ways = HEADS_INTERLEAVED[kind]
    nslots = len(qs) * nblk * (nblk + 1) // 2
    h_spec = pl.BlockSpec((1, hg, s, HEAD_DIM), lambda bi, hi: (bi, hi, 0, 0))
    in_specs = [h_spec] * (len(qs) + 2)
    scratch = [pltpu.VMEM((ways, nslots, BLK, BLK), jnp.float32),
               pltpu.VMEM((ways, nslots * BLK, BLK), jnp.bfloat16),
               pltpu.VMEM((ways, HEAD_DIM, s), jnp.bfloat16)]
    if kind == MOBA:
        scratch.append(pltpu.VMEM((ways, nblk, nblk, BLK), jnp.float32))
    elif kind == FOX:
        in_specs.append(pl.BlockSpec((1, hg, 1, s), lambda bi, hi: (bi, hi, 0, 0)))
        scratch.append(pltpu.VMEM((ways, nblk, BLK, BLK), jnp.float32))
    else:
        in_specs += [pl.BlockSpec((4, DIFF_SUB_DIM), lambda bi, hi: (0, 0)),
                     pl.BlockSpec((1, HEAD_DIM), lambda bi, hi: (0, 0))]
    return pl.pallas_call(
        functools.partial(_attn_kernel, kind, nblk, hg, lambda_init),
        grid=(b, h // hg), in_specs=in_specs,
        out_specs=h_spec,
        out_shape=jax.ShapeDtypeStruct((b, h, s, HEAD_DIM), jnp.bfloat16),
        scratch_shapes=scratch,
        compiler_params=pltpu.CompilerParams(
            dimension_semantics=("arbitrary", "arbitrary"), vmem_limit_bytes=VMEM_LIMIT_BYTES),
        name=f"attn_{KIND_NAMES[kind]}",
    )(*qs, k, v, *extras)


def _gated(y_ref, sz_ref):
    y = jnp.concatenate([y_ref[0, h] for h in range(y_ref.shape[1])], axis=1)
    return (y.astype(jnp.float32) * sz_ref[0].astype(jnp.float32)).astype(jnp.bfloat16)


def _out_kernel(y_ref, sz_ref, x_ref, w_ref, o_ref):
    o_ref[0] = x_ref[0] + _dot(_gated(y_ref, sz_ref), w_ref[...])


def _out_project(y, sz, x, w, tm=1024):
    b, s, d = x.shape
    h = y.shape[1]
    width = h * HEAD_DIM
    row_spec = pl.BlockSpec((1, tm, d), lambda bi, si: (bi, si, 0))
    return pl.pallas_call(
        _out_kernel, grid=(b, s // tm),
        in_specs=[pl.BlockSpec((1, h, tm, HEAD_DIM), lambda bi, si: (bi, 0, si, 0)),
                  pl.BlockSpec((1, tm, width), lambda bi, si: (bi, si, 0)),
                  row_spec, pl.BlockSpec((width, d), lambda bi, si: (0, 0))],
        out_specs=row_spec,
        out_shape=jax.ShapeDtypeStruct((b, s, d), jnp.float32),
        compiler_params=pltpu.CompilerParams(
            dimension_semantics=("arbitrary", "arbitrary"), vmem_limit_bytes=VMEM_LIMIT_BYTES),
        name="out_proj",
    )(y, sz, x, w)


def _rope_tables(seq, dim):
    inv = 1.0 / (ROPE_THETA ** (jnp.arange(0, dim, 2, dtype=jnp.float32) / dim))
    ang = jnp.arange(seq, dtype=jnp.float32)[:, None] * inv[None, :]
    return jnp.cos(ang), jnp.sin(ang)


def kernel(x, norm_g, w_out, a_w_in, a_q_norm, a_k_norm, b_w_in, b_f_bias, b_q_norm, b_k_norm,
           c_w_in, c_q_norm, c_k_norm, c_lambda_q1, c_lambda_k1, c_lambda_q2, c_lambda_k2, c_subln):
    b, s, d = x.shape
    depth = norm_g.shape[0]
    width = N_HEADS * HEAD_DIM
    assert s % BLK == 0 and BLK == MOBA_BLOCK and d == width

    cos_h, sin_h = _rope_tables(s, HEAD_DIM)
    moba_tabs = (jnp.concatenate([cos_h, cos_h], axis=1), jnp.concatenate([-sin_h, sin_h], axis=1))
    cos_d, sin_d = _rope_tables(s, DIFF_SUB_DIM)
    zero = jnp.zeros_like(sin_d)
    diff_tabs = (jnp.concatenate([cos_d] * 4, axis=1),
                 jnp.concatenate([-sin_d, zero, -sin_d, zero], axis=1),
                 jnp.concatenate([zero, sin_d, zero, sin_d], axis=1))

    prev = None
    for i in range(depth):
        kind, j = i % 3, i // 3
        ng = norm_g[i][None, :]
        if kind == MOBA:
            w = a_w_in[j].astype(jnp.bfloat16)
            outs = _project(MOBA, x, ng, w, a_q_norm[j][None, :], a_k_norm[j][None, :], moba_tabs, prev)
        elif kind == FOX:
            w = jnp.pad(b_w_in[j], ((0, 0), (0, LANES - N_HEADS))).astype(jnp.bfloat16)
            fb = jnp.pad(b_f_bias[j], (0, LANES - N_HEADS))[None, :]
            outs = _project(FOX, x, ng, w, b_q_norm[j][None, :], b_k_norm[j][None, :], (fb,), prev)
        else:
            w = c_w_in[j].astype(jnp.bfloat16)
            qg = jnp.concatenate([c_q_norm[j]] * 2)[None, :]
            kg = jnp.concatenate([c_k_norm[j]] * 2)[None, :]
            outs = _project(DIFF, x, ng, w, qg, kg, diff_tabs, prev)
        if prev is not None:
            x, *outs = outs
        if kind == MOBA:
            q, k, v, sz = outs
            y = _attention(MOBA, (q,), k, v, ())
        elif kind == FOX:
            q, k, v, sz, c = outs
            y = _attention(FOX, (q,), k, v, (c,))
        else:
            qa, qb, k, v, sz = outs
            lam = jnp.stack([c_lambda_q1[j], c_lambda_k1[j], c_lambda_q2[j], c_lambda_k2[j]])
            lambda_init = 0.8 - 0.6 * math.exp(-0.3 * i)
            y = _attention(DIFF, (qa, qb), k, v, (lam, c_subln[j][None, :]), lambda_init)
        prev = (y, sz, w_out[i].astype(jnp.bfloat16))
    return _out_project(*prev[:2], x, prev[2])
```

```python
import functools
import math

import jax
import jax.numpy as jnp
from jax import lax
from jax.experimental import pallas as pl
from jax.experimental.pallas import tpu as pltpu

N_HEADS = 8
HEAD_DIM = 128
ROPE_THETA = 10000.0
NORM_EPS = 1e-6
MOBA_BLOCK = 256
MOBA_TOPK = 3
DIFF_SUB_DIM = HEAD_DIM // 2

BLK = 256
QK_LOOKAHEAD = (3, 3, 2)
ATTN_HEADS_PER_STEP = 4
HEADS_INTERLEAVED = (1, 1, 1)
SOFTMAX_BF16 = (True, True, False)
BF16_ROWS = 16
TILE_ORDER = (lambda n: list(range(n)), lambda n: list(range(n)), lambda n: list(range(n - 1, -1, -1)))
LANES = 128
LOG2E = 1.4426950408889634
NEG_INF = float("-inf")
VMEM_LIMIT_BYTES = 56 * 1024 * 1024

MOBA, FOX, DIFF = 0, 1, 2
KIND_NAMES = ("moba", "fox", "diff")


def _dot(a, b):
    return jnp.dot(a, b, preferred_element_type=jnp.float32)


def _dot_nt(a, b):
    return lax.dot_general(a, b, (((1,), (1,)), ((), ())), preferred_element_type=jnp.float32)


def _head_norm_rope(kind, xh, ssq, gain, tabs, scale):
    norm_dim = DIFF_SUB_DIM if kind == DIFF else HEAD_DIM
    y = xh * lax.rsqrt(ssq * (1.0 / norm_dim) + NORM_EPS) * gain
    if kind == MOBA:
        cos_t, sin_t = tabs
        y = y * cos_t + pltpu.roll(y, HEAD_DIM // 2, axis=1) * sin_t
    elif kind == DIFF:
        cos_t, s_up, s_dn = tabs
        y = (y * cos_t + pltpu.roll(y, HEAD_DIM - DIFF_SUB_DIM // 2, axis=1) * s_up
             + pltpu.roll(y, DIFF_SUB_DIM // 2, axis=1) * s_dn)
    if scale != 1.0:
        y = y * scale
    return y


def _proj_kernel(kind, fused, tm, width, *refs):
    if fused:
        y_ref, szp_ref, wout_ref = refs[:3]
        refs = refs[3:]
    if kind == MOBA:
        (x_ref, ng_ref, w_ref, qg_ref, kg_ref, cos_ref, sin_ref, *outs, hdn_ref) = refs
        tabs = (cos_ref[...], sin_ref[...])
    elif kind == FOX:
        (x_ref, ng_ref, w_ref, qg_ref, kg_ref, fb_ref, *outs, hdn_ref, carry_ref) = refs
        tabs = None
    else:
        (x_ref, ng_ref, w_ref, qg_ref, kg_ref, cos_ref, sup_ref, sdn_ref, *outs, hdn_ref) = refs
        tabs = (cos_ref[...], sup_ref[...], sdn_ref[...])
    if fused:
        xnew_ref, *outs = outs
    if kind == MOBA:
        q_ref, k_ref, v_ref, sz_ref = outs
    elif kind == FOX:
        q_ref, k_ref, v_ref, sz_ref, c_ref = outs
    else:
        q_ref, qb_ref, k_ref, v_ref, sz_ref = outs

    if kind == FOX:
        @pl.when(pl.program_id(1) == 0)
        def _():
            carry_ref[...] = jnp.zeros_like(carry_ref)

    x = x_ref[0]
    if fused:
        x = x + _dot(_gated(y_ref, szp_ref), wout_ref[...])
        xnew_ref[0] = x
    ms = jnp.mean(x * x, axis=-1, keepdims=True)
    hdn_ref[...] = (x * lax.rsqrt(ms + NORM_EPS) * ng_ref[...]).astype(jnp.bfloat16)

    norm_dim = DIFF_SUB_DIM if kind == DIFF else HEAD_DIM
    q_scale = (norm_dim ** -0.5) * LOG2E
    pair = 2 * HEAD_DIM
    shift = norm_dim.bit_length() - 1
    gi = lax.broadcasted_iota(jnp.int32, (pair, pair), 0) >> shift
    gj = lax.broadcasted_iota(jnp.int32, (pair, pair), 1) >> shift
    group_ones = jnp.where(gi == gj, 1.0, 0.0).astype(jnp.bfloat16)

    def head_dots(p):
        return tuple(_dot(hdn_ref[...], w_ref[:, g * width + p * pair:g * width + (p + 1) * pair])
                     for g in range(3))

    def head_epilogue(p, rq, rk, rv):
        if kind != FOX:
            ssq_q = _dot((rq * rq).astype(jnp.bfloat16), group_ones)
            ssq_k = _dot((rk * rk).astype(jnp.bfloat16), group_ones)
        for half in range(2):
            h = 2 * p + half
            sl = slice(half * HEAD_DIM, (half + 1) * HEAD_DIM)
            if kind == FOX:
                sq, sk = (jnp.sum(r[:, sl] * r[:, sl], axis=-1, keepdims=True) for r in (rq, rk))
            else:
                sq, sk = ssq_q[:, sl], ssq_k[:, sl]
            q = _head_norm_rope(kind, rq[:, sl], sq, qg_ref[...], tabs, q_scale)
            kk = _head_norm_rope(kind, rk[:, sl], sk, kg_ref[...], tabs, 1.0)
            if kind == DIFF:
                lane = lax.broadcasted_iota(jnp.int32, q.shape, 1)
                lo = lane < DIFF_SUB_DIM
                q_ref[0, h] = jnp.where(lo, q, 0.0).astype(jnp.bfloat16)
                qb_ref[0, h] = jnp.where(lo, 0.0, q).astype(jnp.bfloat16)
            else:
                q_ref[0, h] = q.astype(jnp.bfloat16)
            k_ref[0, h] = kk.astype(jnp.bfloat16)
            v_ref[0, h] = rv[:, sl].astype(jnp.bfloat16)

    def forget_gates(fl):
        head_lanes = lax.broadcasted_iota(jnp.int32, fl.shape, 1) < N_HEADS
        logf = jnp.minimum(fl, 0.0) - jnp.log(1.0 + jnp.exp(-jnp.abs(fl)))
        logf = jnp.where(head_lanes, logf, 0.0)
        row = lax.broadcasted_iota(jnp.int32, (tm, tm), 0)
        col = lax.broadcasted_iota(jnp.int32, (tm, tm), 1)
        tril = jnp.where(row >= col, 1.0, 0.0).astype(jnp.bfloat16)
        hi = logf.astype(jnp.bfloat16).astype(jnp.float32)
        r1 = logf - hi
        mid = r1.astype(jnp.bfloat16).astype(jnp.float32)
        lo_ = (r1 - mid).astype(jnp.bfloat16).astype(jnp.float32)
        parts = hi + pltpu.roll(mid, N_HEADS, axis=1) + pltpu.roll(lo_, 2 * N_HEADS, axis=1)
        cs = _dot(tril, parts.astype(jnp.bfloat16))
        csum = cs + pltpu.roll(cs, LANES - N_HEADS, axis=1) + pltpu.roll(cs, LANES - 2 * N_HEADS, axis=1)
        csum = jnp.where(head_lanes, csum, 0.0)
        c_tile = csum + carry_ref[0:1, :]
        carry_ref[0:1, :] = c_tile[tm - 1:tm, :]
        cT = (c_tile * LOG2E).T
        for h in range(N_HEADS):
            c_ref[0, h] = cT[h:h + 1, :]

    if kind == FOX:
        fl = _dot(hdn_ref[...], w_ref[:, 4 * width:4 * width + LANES]) + fb_ref[...]
    prev = None
    for p in range(N_HEADS // 2):
        cur = head_dots(p)
        if kind == FOX and p == 0:
            forget_gates(fl)
        if prev is not None:
            head_epilogue(p - 1, *prev)
        prev = cur
    zc = width // 2
    rz = [None, None]
    rz[0] = _dot(hdn_ref[...], w_ref[:, 3 * width:3 * width + zc])
    head_epilogue(N_HEADS // 2 - 1, *prev)
    rz[1] = _dot(hdn_ref[...], w_ref[:, 3 * width + zc:4 * width])
    for c in range(2):
        sz_ref[0, :, c * zc:(c + 1) * zc] = (rz[c] * (1.0 / (1.0 + jnp.exp(-rz[c])))).astype(jnp.bfloat16)


def _project(kind, x, ng, w, qg, kg, extras, prev=None, tm=512):
    b, s, d = x.shape
    width = N_HEADS * HEAD_DIM
    grid = (b, s // tm)
    fused = prev is not None

    def const(shape):
        return pl.BlockSpec(shape, lambda bi, si: (0,) * len(shape))

    tab_spec = pl.BlockSpec((tm, HEAD_DIM), lambda bi, si: (si, 0))
    x_spec = pl.BlockSpec((1, tm, d), lambda bi, si: (bi, si, 0))
    h_spec = pl.BlockSpec((1, N_HEADS, tm, HEAD_DIM), lambda bi, si: (bi, 0, si, 0))
    sz_spec = pl.BlockSpec((1, tm, width), lambda bi, si: (bi, si, 0))
    in_specs = [h_spec, sz_spec, const((width, d))] if fused else []
    in_specs += [x_spec, const((1, d)), const(w.shape), const((1, HEAD_DIM)), const((1, HEAD_DIM))]
    if kind == MOBA:
        in_specs += [tab_spec, tab_spec]
    elif kind == FOX:
        in_specs += [const((1, LANES))]
    else:
        in_specs += [tab_spec, tab_spec, tab_spec]

    h_shape = jax.ShapeDtypeStruct((b, N_HEADS, s, HEAD_DIM), jnp.bfloat16)
    sz_shape = jax.ShapeDtypeStruct((b, s, width), jnp.bfloat16)

    n_heads_out = 4 if kind == DIFF else 3
    out_shape = [h_shape] * n_heads_out + [sz_shape]
    out_specs = [h_spec] * n_heads_out + [sz_spec]
    if fused:
        out_shape.insert(0, jax.ShapeDtypeStruct((b, s, d), jnp.float32))
        out_specs.insert(0, x_spec)
    scratch = [pltpu.VMEM((tm, d), jnp.bfloat16)]
    if kind == FOX:
        out_shape.append(jax.ShapeDtypeStruct((b, N_HEADS, 1, s), jnp.float32))
        out_specs.append(pl.BlockSpec((1, N_HEADS, 1, tm), lambda bi, si: (bi, 0, 0, si)))
        scratch.append(pltpu.VMEM((8, LANES), jnp.float32))

    return pl.pallas_call(
        functools.partial(_proj_kernel, kind, fused, tm, width),
        grid=grid, in_specs=in_specs, out_specs=out_specs, out_shape=out_shape,
        scratch_shapes=scratch,
        compiler_params=pltpu.CompilerParams(
            dimension_semantics=("arbitrary", "arbitrary"), vmem_limit_bytes=VMEM_LIMIT_BYTES),
        name=f"proj_{KIND_NAMES[kind]}",
    )(*(prev or ()), x, ng, w, qg, kg, *extras)


def _attn_kernel(kind, nblk, heads, lambda_init, *refs):
    if kind == MOBA:
        q_ref, k_ref, v_ref, o_ref, s_ref, p_ref, vT_ref, bias_ref = refs
        q_refs = (q_ref,)
    elif kind == FOX:
        q_ref, k_ref, v_ref, c_ref, o_ref, s_ref, p_ref, vT_ref, ck_ref = refs
        q_refs = (q_ref,)
    else:
        q_ref, qb_ref, k_ref, v_ref, lam_ref, sg_ref, o_ref, s_ref, p_ref, vT_ref = refs
        q_refs = (q_ref, qb_ref)
    ways = HEADS_INTERLEAVED[kind]

    def group(g, carry):
        stages = [_attn_head_stages(kind, nblk, lambda_init, refs, q_refs, w, g * ways + w) for w in range(ways)]
        pending = [{} for _ in range(ways)]
        lookahead = QK_LOOKAHEAD[kind]
        order = TILE_ORDER[kind](nblk)
        for step in range(nblk + lookahead):
            for w, (scores, finish) in enumerate(stages):
                if step < nblk:
                    pending[w][order[step]] = scores(order[step])
            done = step - lookahead
            for w, (scores, finish) in enumerate(stages):
                if done >= 0:
                    finish(order[done], pending[w].pop(order[done]))
        return carry

    lax.fori_loop(0, heads // ways, group, 0)


def _attn_head_stages(kind, nblk, lambda_init, refs, q_refs, w, hh):
    if kind == MOBA:
        _, k_ref, v_ref, o_ref, s_ref, p_ref, vT_ref, bias_ref = refs
        bias_ref = bias_ref.at[w]
    elif kind == FOX:
        _, k_ref, v_ref, c_ref, o_ref, s_ref, p_ref, vT_ref, ck_ref = refs
        ck_ref = ck_ref.at[w]
        cv = c_ref.at[0, hh]
    else:
        _, _, k_ref, v_ref, lam_ref, sg_ref, o_ref, s_ref, p_ref, vT_ref = refs
    s_ref, p_ref, vT_ref = s_ref.at[w], p_ref.at[w], vT_ref.at[w]
    nvar = len(q_refs)
    qv = [r.at[0, hh] for r in q_refs]
    kv = k_ref.at[0, hh]
    ov = o_ref.at[0, hh]

    vT_ref[0:HEAD_DIM, :] = v_ref[0, hh].T
    if SOFTMAX_BF16[kind]:
        first = lax.broadcasted_iota(jnp.int32, (BF16_ROWS, vT_ref.shape[1]), 0) == 0
        vT_ref[HEAD_DIM:HEAD_DIM + BF16_ROWS, :] = jnp.where(first, 1.0, 0.0).astype(jnp.bfloat16)

    if kind == MOBA:
        kmean = jnp.concatenate(
            [jnp.sum(kv[j * BLK:(j + 1) * BLK, :].astype(jnp.float32), axis=0, keepdims=True)
             for j in range(nblk)], axis=0)
        kmean = (kmean * (1.0 / MOBA_BLOCK)).astype(jnp.bfloat16)
        row = lax.broadcasted_iota(jnp.int32, (nblk, BLK), 0)
        for t in range(1, nblk):
            gate = _dot_nt(kmean, qv[0][t * BLK:(t + 1) * BLK, :])
            past = row < t
            gm = jnp.where(past, gate, NEG_INF)
            cnt = jnp.zeros((nblk, BLK), jnp.int32)
            for r in range(1, nblk):
                other = pltpu.roll(gm, r, axis=0)
                tie_first = jnp.where(row >= r, 1, 0)
                cnt = cnt + jnp.where(other > gm, 1, jnp.where(other == gm, tie_first, 0))
            bias_ref[t] = jnp.where(past, jnp.where(cnt < MOBA_TOPK, 0.0, NEG_INF), NEG_INF)
    elif kind == FOX:
        for j in range(nblk):
            ck_ref[j] = jnp.broadcast_to(cv[:, j * BLK:(j + 1) * BLK], (BLK, BLK)).T

    krow = lax.broadcasted_iota(jnp.int32, (BLK, BLK), 0)
    qcol = lax.broadcasted_iota(jnp.int32, (BLK, BLK), 1)
    causal = krow <= qcol

    def base_slot(qi, v):
        return nvar * qi * (qi + 1) // 2 + v * (qi + 1)

    def scores(qi):
        nk = qi + 1
        q_sl = slice(qi * BLK, (qi + 1) * BLK)
        k_all = kv[0:nk * BLK, :]
        shifts = []
        for v in range(nvar):
            base = base_slot(qi, v)
            s = _dot_nt(k_all, qv[v][q_sl, :])
            m = None
            for j in range(nk):
                sj = s[j * BLK:(j + 1) * BLK]
                if kind == FOX:
                    sj = sj - ck_ref[j]
                if j == qi:
                    sj = jnp.where(causal, sj, NEG_INF)
                s_ref[base + j] = sj
                cm = jnp.max(sj, axis=0, keepdims=True)
                if kind == MOBA and j < qi:
                    cm = cm + bias_ref[qi, j:j + 1, :]
                m = cm if m is None else jnp.maximum(m, cm)
            if kind == FOX:
                cq = cv[:, q_sl]
                shifts.append(cq - (m + cq))
            else:
                shifts.append(-m)
        return shifts

    def finish(qi, shifts):
        nk = qi + 1
        q_sl = slice(qi * BLK, (qi + 1) * BLK)
        v_all = vT_ref[:, 0:nk * BLK]
        if kind == DIFF:
            sums = []
            for v in range(nvar):
                base = base_slot(qi, v)
                l = None
                for j in range(nk):
                    ej = jnp.exp2(s_ref[base + j] + shifts[v])
                    lj = jnp.sum(ej, axis=0, keepdims=True)
                    l = lj if l is None else l + lj
                    s_ref[base + j] = ej
                sums.append(l)
            lam = (jnp.exp(jnp.sum(lam_ref[0:1, :] * lam_ref[1:2, :], axis=1, keepdims=True))
                   - jnp.exp(jnp.sum(lam_ref[2:3, :] * lam_ref[3:4, :], axis=1, keepdims=True))
                   + lambda_init)
            w1 = 1.0 / sums[0]
            w2 = lam / sums[1]
            b1, b2 = base_slot(qi, 0), base_slot(qi, 1)
            for j in range(nk):
                p_ref[(b1 + j) * BLK:(b1 + j + 1) * BLK, :] = (
                    s_ref[b1 + j] * w1 - s_ref[b2 + j] * w2).astype(jnp.bfloat16)
            y = _dot(v_all, p_ref[b1 * BLK:(b1 + nk) * BLK, :]).T
            msq = jnp.mean(y * y, axis=-1, keepdims=True)
            y = y * lax.rsqrt(msq + NORM_EPS) * sg_ref[...] * (1.0 - lambda_init)
            ov[q_sl, :] = y.astype(jnp.bfloat16)
            return
        outs = []
        for v in range(nvar):
            base = base_slot(qi, v)
            l = None
            for j in range(nk):
                sh = shifts[v] + bias_ref[qi, j:j + 1, :] if (kind == MOBA and j < qi) else shifts[v]
                rows = slice((base + j) * BLK, (base + j + 1) * BLK)
                if SOFTMAX_BF16[kind]:
                    p_ref[rows, :] = jnp.exp2((s_ref[base + j] + sh).astype(jnp.bfloat16))
                else:
                    pj = jnp.exp2(s_ref[base + j] + sh)
                    lj = jnp.sum(pj, axis=0, keepdims=True)
                    l = lj if l is None else l + lj
                    p_ref[rows, :] = pj.astype(jnp.bfloat16)
            acc = _dot(v_all, p_ref[base * BLK:(base + nk) * BLK, :])
            if SOFTMAX_BF16[kind]:
                l = acc[HEAD_DIM:HEAD_DIM + 1, :]
                acc = acc[0:HEAD_DIM, :]
            outs.append(acc * (1.0 / l))
        ov[q_sl, :] = outs[0].T.astype(jnp.bfloat16)

    return scores, finish


def _attention(kind, qs, k, v, extras, lambda_init=0.0):
    b, h, s, _ = k.shape
    nblk = s // BLK
    hg = ATTN_HEADS_PER_STEP
    ways = HEADS_INTERLEAVED[kind]
    nslots = len(qs) * nblk * (nblk + 1) // 2
    h_spec = pl.BlockSpec((1, hg, s, HEAD_DIM), lambda bi, hi: (bi, hi, 0, 0))
    in_specs = [h_spec] * (len(qs) + 2)
    scratch = [pltpu.VMEM((ways, nslots, BLK, BLK), jnp.float32),
               pltpu.VMEM((ways, nslots * BLK, BLK), jnp.bfloat16),
               pltpu.VMEM((ways, HEAD_DIM + (BF16_ROWS if SOFTMAX_BF16[kind] else 0), s), jnp.bfloat16)]
    if kind == MOBA:
        scratch.append(pltpu.VMEM((ways, nblk, nblk, BLK), jnp.float32))
    elif kind == FOX:
        in_specs.append(pl.BlockSpec((1, hg, 1, s), lambda bi, hi: (bi, hi, 0, 0)))
        scratch.append(pltpu.VMEM((ways, nblk, BLK, BLK), jnp.float32))
    else:
        in_specs += [pl.BlockSpec((4, DIFF_SUB_DIM), lambda bi, hi: (0, 0)),
                     pl.BlockSpec((1, HEAD_DIM), lambda bi, hi: (0, 0))]
    return pl.pallas_call(
        functools.partial(_attn_kernel, kind, nblk, hg, lambda_init),
        grid=(b, h // hg), in_specs=in_specs,
        out_specs=h_spec,
        out_shape=jax.ShapeDtypeStruct((b, h, s, HEAD_DIM), jnp.bfloat16),
        scratch_shapes=scratch,
        compiler_params=pltpu.CompilerParams(
            dimension_semantics=("arbitrary", "arbitrary"), vmem_limit_bytes=VMEM_LIMIT_BYTES),
        name=f"attn_{KIND_NAMES[kind]}",
    )(*qs, k, v, *extras)


def _gated(y_ref, sz_ref):
    y = jnp.concatenate([y_ref[0, h] for h in range(y_ref.shape[1])], axis=1)
    return (y.astype(jnp.float32) * sz_ref[0].astype(jnp.float32)).astype(jnp.bfloat16)


def _out_kernel(y_ref, sz_ref, x_ref, w_ref, o_ref):
    o_ref[0] = x_ref[0] + _dot(_gated(y_ref, sz_ref), w_ref[...])


def _out_project(y, sz, x, w, tm=1024):
    b, s, d = x.shape
    h = y.shape[1]
    width = h * HEAD_DIM
    row_spec = pl.BlockSpec((1, tm, d), lambda bi, si: (bi, si, 0))
    return pl.pallas_call(
        _out_kernel, grid=(b, s // tm),
        in_specs=[pl.BlockSpec((1, h, tm, HEAD_DIM), lambda bi, si: (bi, 0, si, 0)),
                  pl.BlockSpec((1, tm, width), lambda bi, si: (bi, si, 0)),
                  row_spec, pl.BlockSpec((width, d), lambda bi, si: (0, 0))],
        out_specs=row_spec,
        out_shape=jax.ShapeDtypeStruct((b, s, d), jnp.float32),
        compiler_params=pltpu.CompilerParams(
            dimension_semantics=("arbitrary", "arbitrary"), vmem_limit_bytes=VMEM_LIMIT_BYTES),
        name="out_proj",
    )(y, sz, x, w)


def _rope_tables(seq, dim):
    inv = 1.0 / (ROPE_THETA ** (jnp.arange(0, dim, 2, dtype=jnp.float32) / dim))
    ang = jnp.arange(seq, dtype=jnp.float32)[:, None] * inv[None, :]
    return jnp.cos(ang), jnp.sin(ang)


def kernel(x, norm_g, w_out, a_w_in, a_q_norm, a_k_norm, b_w_in, b_f_bias, b_q_norm, b_k_norm,
           c_w_in, c_q_norm, c_k_norm, c_lambda_q1, c_lambda_k1, c_lambda_q2, c_lambda_k2, c_subln):
    b, s, d = x.shape
    depth = norm_g.shape[0]
    width = N_HEADS * HEAD_DIM
    assert s % BLK == 0 and BLK == MOBA_BLOCK and d == width

    cos_h, sin_h = _rope_tables(s, HEAD_DIM)
    moba_tabs = (jnp.concatenate([cos_h, cos_h], axis=1), jnp.concatenate([-sin_h, sin_h], axis=1))
    cos_d, sin_d = _rope_tables(s, DIFF_SUB_DIM)
    zero = jnp.zeros_like(sin_d)
    diff_tabs = (jnp.concatenate([cos_d] * 4, axis=1),
                 jnp.concatenate([-sin_d, zero, -sin_d, zero], axis=1),
                 jnp.concatenate([zero, sin_d, zero, sin_d], axis=1))

    prev = None
    for i in range(depth):
        kind, j = i % 3, i // 3
        ng = norm_g[i][None, :]
        if kind == MOBA:
            w = a_w_in[j].astype(jnp.bfloat16)
            outs = _project(MOBA, x, ng, w, a_q_norm[j][None, :], a_k_norm[j][None, :], moba_tabs, prev)
        elif kind == FOX:
            w = jnp.pad(b_w_in[j], ((0, 0), (0, LANES - N_HEADS))).astype(jnp.bfloat16)
            fb = jnp.pad(b_f_bias[j], (0, LANES - N_HEADS))[None, :]
            outs = _project(FOX, x, ng, w, b_q_norm[j][None, :], b_k_norm[j][None, :], (fb,), prev)
        else:
            w = c_w_in[j].astype(jnp.bfloat16)
            qg = jnp.concatenate([c_q_norm[j]] * 2)[None, :]
            kg = jnp.concatenate([c_k_norm[j]] * 2)[None, :]
            outs = _project(DIFF, x, ng, w, qg, kg, diff_tabs, prev)
        if prev is not None:
            x, *outs = outs
        if kind == MOBA:
            q, k, v, sz = outs
            y = _attention(MOBA, (q,), k, v, ())
        elif kind == FOX:
            q, k, v, sz, c = outs
            y = _attention(FOX, (q,), k, v, (c,))
        else:
            qa, qb, k, v, sz = outs
            lam = jnp.stack([c_lambda_q1[j], c_lambda_k1[j], c_lambda_q2[j], c_lambda_k2[j]])
            lambda_init = 0.8 - 0.6 * math.exp(-0.3 * i)
            y = _attention(DIFF, (qa, qb), k, v, (lam, c_subln[j][None, :]), lambda_init)
        prev = (y, sz, w_out[i].astype(jnp.bfloat16))
    return _out_project(*prev[:2], x, prev[2])
```

```python
import functools
import math

import jax
import jax.numpy as jnp
from jax import lax
from jax.experimental import pallas as pl
from jax.experimental.pallas import tpu as pltpu

N_HEADS = 8
HEAD_DIM = 128
ROPE_THETA = 10000.0
NORM_EPS = 1e-6
MOBA_BLOCK = 256
MOBA_TOPK = 3
DIFF_SUB_DIM = HEAD_DIM // 2

BLK = 256
QK_LOOKAHEAD = (3, 3, 2)
ATTN_HEADS_PER_STEP = 4
HEADS_INTERLEAVED = (1, 2, 1)
TILE_ORDER = (lambda n: list(range(n)), lambda n: list(range(n)), lambda n: list(range(n - 1, -1, -1)))
LANES = 128
LOG2E = 1.4426950408889634
NEG_INF = float("-inf")
VMEM_LIMIT_BYTES = 56 * 1024 * 1024

MOBA, FOX, DIFF = 0, 1, 2
KIND_NAMES = ("moba", "fox", "diff")


def _dot(a, b):
    return jnp.dot(a, b, preferred_element_type=jnp.float32)


def _dot_nt(a, b):
    return lax.dot_general(a, b, (((1,), (1,)), ((), ())), preferred_element_type=jnp.float32)


def _head_norm_rope(kind, xh, ssq, gain, tabs, scale):
    norm_dim = DIFF_SUB_DIM if kind == DIFF else HEAD_DIM
    y = xh * lax.rsqrt(ssq * (1.0 / norm_dim) + NORM_EPS) * gain
    if kind == MOBA:
        cos_t, sin_t = tabs
        y = y * cos_t + pltpu.roll(y, HEAD_DIM // 2, axis=1) * sin_t
    elif kind == DIFF:
        cos_t, s_up, s_dn = tabs
        y = (y * cos_t + pltpu.roll(y, HEAD_DIM - DIFF_SUB_DIM // 2, axis=1) * s_up
             + pltpu.roll(y, DIFF_SUB_DIM // 2, axis=1) * s_dn)
    if scale != 1.0:
        y = y * scale
    return y


def _proj_kernel(kind, fused, tm, width, *refs):
    if fused:
        y_ref, szp_ref, wout_ref = refs[:3]
        refs = refs[3:]
    if kind == MOBA:
        (x_ref, ng_ref, w_ref, qg_ref, kg_ref, cos_ref, sin_ref, *outs, hdn_ref) = refs
        tabs = (cos_ref[...], sin_ref[...])
    elif kind == FOX:
        (x_ref, ng_ref, w_ref, qg_ref, kg_ref, fb_ref, *outs, hdn_ref, carry_ref) = refs
        tabs = None
    else:
        (x_ref, ng_ref, w_ref, qg_ref, kg_ref, cos_ref, sup_ref, sdn_ref, *outs, hdn_ref) = refs
        tabs = (cos_ref[...], sup_ref[...], sdn_ref[...])
    if fused:
        xnew_ref, *outs = outs
    if kind == MOBA:
        q_ref, k_ref, v_ref, sz_ref = outs
    elif kind == FOX:
        q_ref, k_ref, v_ref, sz_ref, c_ref = outs
    else:
        q_ref, qb_ref, k_ref, v_ref, sz_ref = outs

    if kind == FOX:
        @pl.when(pl.program_id(1) == 0)
        def _():
            carry_ref[...] = jnp.zeros_like(carry_ref)

    x = x_ref[0]
    if fused:
        x = x + _dot(_gated(y_ref, szp_ref), wout_ref[...])
        xnew_ref[0] = x
    ms = jnp.mean(x * x, axis=-1, keepdims=True)
    hdn_ref[...] = (x * lax.rsqrt(ms + NORM_EPS) * ng_ref[...]).astype(jnp.bfloat16)

    norm_dim = DIFF_SUB_DIM if kind == DIFF else HEAD_DIM
    q_scale = (norm_dim ** -0.5) * LOG2E
    pair = 2 * HEAD_DIM
    shift = norm_dim.bit_length() - 1
    gi = lax.broadcasted_iota(jnp.int32, (pair, pair), 0) >> shift
    gj = lax.broadcasted_iota(jnp.int32, (pair, pair), 1) >> shift
    group_ones = jnp.where(gi == gj, 1.0, 0.0).astype(jnp.bfloat16)

    def pair_dot(g, p):
        return _dot(hdn_ref[...], w_ref[:, g * width + p * pair:g * width + (p + 1) * pair])

    def v_store(p, rv):
        for half in range(2):
            v_ref[0, 2 * p + half] = rv[:, half * HEAD_DIM:(half + 1) * HEAD_DIM].astype(jnp.bfloat16)

    def qk_epilogue(p, rq, rk):
        if kind != FOX:
            ssq_q = _dot((rq * rq).astype(jnp.bfloat16), group_ones)
            ssq_k = _dot((rk * rk).astype(jnp.bfloat16), group_ones)
        for half in range(2):
            h = 2 * p + half
            sl = slice(half * HEAD_DIM, (half + 1) * HEAD_DIM)
            if kind == FOX:
                sq, sk = (jnp.sum(r[:, sl] * r[:, sl], axis=-1, keepdims=True) for r in (rq, rk))
            else:
                sq, sk = ssq_q[:, sl], ssq_k[:, sl]
            q = _head_norm_rope(kind, rq[:, sl], sq, qg_ref[...], tabs, q_scale)
            kk = _head_norm_rope(kind, rk[:, sl], sk, kg_ref[...], tabs, 1.0)
            if kind == DIFF:
                lane = lax.broadcasted_iota(jnp.int32, q.shape, 1)
                lo = lane < DIFF_SUB_DIM
                q_ref[0, h] = jnp.where(lo, q, 0.0).astype(jnp.bfloat16)
                qb_ref[0, h] = jnp.where(lo, 0.0, q).astype(jnp.bfloat16)
            else:
                q_ref[0, h] = q.astype(jnp.bfloat16)
            k_ref[0, h] = kk.astype(jnp.bfloat16)

    def forget_gates(fl):
        head_lanes = lax.broadcasted_iota(jnp.int32, fl.shape, 1) < N_HEADS
        logf = jnp.minimum(fl, 0.0) - jnp.log(1.0 + jnp.exp(-jnp.abs(fl)))
        logf = jnp.where(head_lanes, logf, 0.0)
        row = lax.broadcasted_iota(jnp.int32, (tm, tm), 0)
        col = lax.broadcasted_iota(jnp.int32, (tm, tm), 1)
        tril = jnp.where(row >= col, 1.0, 0.0).astype(jnp.bfloat16)
        hi = logf.astype(jnp.bfloat16).astype(jnp.float32)
        r1 = logf - hi
        mid = r1.astype(jnp.bfloat16).astype(jnp.float32)
        lo_ = (r1 - mid).astype(jnp.bfloat16).astype(jnp.float32)
        parts = hi + pltpu.roll(mid, N_HEADS, axis=1) + pltpu.roll(lo_, 2 * N_HEADS, axis=1)
        cs = _dot(tril, parts.astype(jnp.bfloat16))
        csum = cs + pltpu.roll(cs, LANES - N_HEADS, axis=1) + pltpu.roll(cs, LANES - 2 * N_HEADS, axis=1)
        csum = jnp.where(head_lanes, csum, 0.0)
        c_tile = csum + carry_ref[0:1, :]
        carry_ref[0:1, :] = c_tile[tm - 1:tm, :]
        cT = (c_tile * LOG2E).T
        for h in range(N_HEADS):
            c_ref[0, h] = cT[h:h + 1, :]

    if kind == FOX:
        fl = _dot(hdn_ref[...], w_ref[:, 4 * width:4 * width + LANES]) + fb_ref[...]
    n_pairs = N_HEADS // 2
    prev = None
    for p in range(n_pairs):
        cur = (pair_dot(0, p), pair_dot(1, p))
        v_store(p, pair_dot(2, p))
        if kind == FOX and p == 0:
            forget_gates(fl)
        if prev is not None:
            qk_epilogue(p - 1, *prev)
        prev = cur
    zc = width // 2
    rz = [None, None]
    rz[0] = _dot(hdn_ref[...], w_ref[:, 3 * width:3 * width + zc])
    qk_epilogue(n_pairs - 1, *prev)
    rz[1] = _dot(hdn_ref[...], w_ref[:, 3 * width + zc:4 * width])
    for c in range(2):
        sz_ref[0, :, c * zc:(c + 1) * zc] = (rz[c] * (1.0 / (1.0 + jnp.exp(-rz[c])))).astype(jnp.bfloat16)


def _project(kind, x, ng, w, qg, kg, extras, prev=None, tm=512):
    b, s, d = x.shape
    width = N_HEADS * HEAD_DIM
    grid = (b, s // tm)
    fused = prev is not None

    def const(shape):
        return pl.BlockSpec(shape, lambda bi, si: (0,) * len(shape))

    tab_spec = pl.BlockSpec((tm, HEAD_DIM), lambda bi, si: (si, 0))
    x_spec = pl.BlockSpec((1, tm, d), lambda bi, si: (bi, si, 0))
    h_spec = pl.BlockSpec((1, N_HEADS, tm, HEAD_DIM), lambda bi, si: (bi, 0, si, 0))
    sz_spec = pl.BlockSpec((1, tm, width), lambda bi, si: (bi, si, 0))
    in_specs = [h_spec, sz_spec, const((width, d))] if fused else []
    in_specs += [x_spec, const((1, d)), const(w.shape), const((1, HEAD_DIM)), const((1, HEAD_DIM))]
    if kind == MOBA:
        in_specs += [tab_spec, tab_spec]
    elif kind == FOX:
        in_specs += [const((1, LANES))]
    else:
        in_specs += [tab_spec, tab_spec, tab_spec]

    h_shape = jax.ShapeDtypeStruct((b, N_HEADS, s, HEAD_DIM), jnp.bfloat16)
    sz_shape = jax.ShapeDtypeStruct((b, s, width), jnp.bfloat16)

    n_heads_out = 4 if kind == DIFF else 3
    out_shape = [h_shape] * n_heads_out + [sz_shape]
    out_specs = [h_spec] * n_heads_out + [sz_spec]
    if fused:
        out_shape.insert(0, jax.ShapeDtypeStruct((b, s, d), jnp.float32))
        out_specs.insert(0, x_spec)
    scratch = [pltpu.VMEM((tm, d), jnp.bfloat16)]
    if kind == FOX:
        out_shape.append(jax.ShapeDtypeStruct((b, N_HEADS, 1, s), jnp.float32))
        out_specs.append(pl.BlockSpec((1, N_HEADS, 1, tm), lambda bi, si: (bi, 0, 0, si)))
        scratch.append(pltpu.VMEM((8, LANES), jnp.float32))

    return pl.pallas_call(
        functools.partial(_proj_kernel, kind, fused, tm, width),
        grid=grid, in_specs=in_specs, out_specs=out_specs, out_shape=out_shape,
        scratch_shapes=scratch,
        compiler_params=pltpu.CompilerParams(
            dimension_semantics=("arbitrary", "arbitrary"), vmem_limit_bytes=VMEM_LIMIT_BYTES),
        name=f"proj_{KIND_NAMES[kind]}",
    )(*(prev or ()), x, ng, w, qg, kg, *extras)


def _attn_kernel(kind, nblk, heads, lambda_init, *refs):
    if kind == MOBA:
        q_ref, k_ref, v_ref, o_ref, s_ref, p_ref, vT_ref, bias_ref = refs
        q_refs = (q_ref,)
    elif kind == FOX:
        q_ref, k_ref, v_ref, c_ref, o_ref, s_ref, p_ref, vT_ref, ck_ref = refs
        q_refs = (q_ref,)
    else:
        q_ref, qb_ref, k_ref, v_ref, lam_ref, sg_ref, o_ref, s_ref, p_ref, vT_ref = refs
        q_refs = (q_ref, qb_ref)
    ways = HEADS_INTERLEAVED[kind]

    def group(g, carry):
        stages = [_attn_head_stages(kind, nblk, lambda_init, refs, q_refs, w, g * ways + w) for w in range(ways)]
        pending = [{} for _ in range(ways)]
        lookahead = QK_LOOKAHEAD[kind]
        order = TILE_ORDER[kind](nblk)
        for step in range(nblk + lookahead):
            for w, (scores, finish) in enumerate(stages):
                if step < nblk:
                    pending[w][order[step]] = scores(order[step])
            done = step - lookahead
            for w, (scores, finish) in enumerate(stages):
                if done >= 0:
                    finish(order[done], pending[w].pop(order[done]))
        return carry

    lax.fori_loop(0, heads // ways, group, 0)


def _attn_head_stages(kind, nblk, lambda_init, refs, q_refs, w, hh):
    if kind == MOBA:
        _, k_ref, v_ref, o_ref, s_ref, p_ref, vT_ref, bias_ref = refs
        bias_ref = bias_ref.at[w]
    elif kind == FOX:
        _, k_ref, v_ref, c_ref, o_ref, s_ref, p_ref, vT_ref, ck_ref = refs
        ck_ref = ck_ref.at[w]
        cv = c_ref.at[0, hh]
    else:
        _, _, k_ref, v_ref, lam_ref, sg_ref, o_ref, s_ref, p_ref, vT_ref = refs
    s_ref, p_ref, vT_ref = s_ref.at[w], p_ref.at[w], vT_ref.at[w]
    nvar = len(q_refs)
    qv = [r.at[0, hh] for r in q_refs]
    kv = k_ref.at[0, hh]
    ov = o_ref.at[0, hh]

    vT_ref[...] = v_ref[0, hh].T

    if kind == MOBA:
        kmean = jnp.concatenate(
            [jnp.sum(kv[j * BLK:(j + 1) * BLK, :].astype(jnp.float32), axis=0, keepdims=True)
             for j in range(nblk)], axis=0)
        kmean = (kmean * (1.0 / MOBA_BLOCK)).astype(jnp.bfloat16)
        row = lax.broadcasted_iota(jnp.int32, (nblk, BLK), 0)
        for t in range(1, nblk):
            gate = _dot_nt(kmean, qv[0][t * BLK:(t + 1) * BLK, :])
            past = row < t
            gm = jnp.where(past, gate, NEG_INF)
            cnt = jnp.zeros((nblk, BLK), jnp.int32)
            for r in range(1, nblk):
                other = pltpu.roll(gm, r, axis=0)
                tie_first = jnp.where(row >= r, 1, 0)
                cnt = cnt + jnp.where(other > gm, 1, jnp.where(other == gm, tie_first, 0))
            bias_ref[t] = jnp.where(past, jnp.where(cnt < MOBA_TOPK, 0.0, NEG_INF), NEG_INF)
    elif kind == FOX:
        for j in range(nblk):
            ck_ref[j] = jnp.broadcast_to(cv[:, j * BLK:(j + 1) * BLK], (BLK, BLK)).T

    krow = lax.broadcasted_iota(jnp.int32, (BLK, BLK), 0)
    qcol = lax.broadcasted_iota(jnp.int32, (BLK, BLK), 1)
    causal = krow <= qcol

    def base_slot(qi, v):
        return nvar * qi * (qi + 1) // 2 + v * (qi + 1)

    def scores(qi):
        nk = qi + 1
        q_sl = slice(qi * BLK, (qi + 1) * BLK)
        k_all = kv[0:nk * BLK, :]
        shifts = []
        for v in range(nvar):
            base = base_slot(qi, v)
            s = _dot_nt(k_all, qv[v][q_sl, :])
            m = None
            for j in range(nk):
                sj = s[j * BLK:(j + 1) * BLK]
                if kind == FOX:
                    sj = sj - ck_ref[j]
                if j == qi:
                    sj = jnp.where(causal, sj, NEG_INF)
                s_ref[base + j] = sj
                cm = jnp.max(sj, axis=0, keepdims=True)
                if kind == MOBA and j < qi:
                    cm = cm + bias_ref[qi, j:j + 1, :]
                m = cm if m is None else jnp.maximum(m, cm)
            if kind == FOX:
                cq = cv[:, q_sl]
                shifts.append(cq - (m + cq))
            else:
                shifts.append(-m)
        return shifts

    def finish(qi, shifts):
        nk = qi + 1
        q_sl = slice(qi * BLK, (qi + 1) * BLK)
        v_all = vT_ref[:, 0:nk * BLK]
        outs = []
        for v in range(nvar):
            base = base_slot(qi, v)
            l = None
            for j in range(nk):
                sh = shifts[v] + bias_ref[qi, j:j + 1, :] if (kind == MOBA and j < qi) else shifts[v]
                pj = jnp.exp2(s_ref[base + j] + sh)
                lj = jnp.sum(pj, axis=0, keepdims=True)
                l = lj if l is None else l + lj
                p_ref[(base + j) * BLK:(base + j + 1) * BLK, :] = pj.astype(jnp.bfloat16)
            acc = _dot(v_all, p_ref[base * BLK:(base + nk) * BLK, :])
            outs.append(acc * (1.0 / l))
        if kind == DIFF:
            lam = (jnp.exp(jnp.sum(lam_ref[0:1, :] * lam_ref[1:2, :], axis=1, keepdims=True))
                   - jnp.exp(jnp.sum(lam_ref[2:3, :] * lam_ref[3:4, :], axis=1, keepdims=True))
                   + lambda_init)
            y = (outs[0] - lam * outs[1]).T
            msq = jnp.mean(y * y, axis=-1, keepdims=True)
            y = y * lax.rsqrt(msq + NORM_EPS) * sg_ref[...] * (1.0 - lambda_init)
        else:
            y = outs[0].T
        ov[q_sl, :] = y.astype(jnp.bfloat16)

    return scores, finish


def _attention(kind, qs, k, v, extras, lambda_init=0.0):
    b, h, s, _ = k.shape
    nblk = s // BLK
    hg = ATTN_HEADS_PER_STEP
    ways = HEADS_INTERLEAVED[kind]
    nslots = len(qs) * nblk * (nblk + 1) // 2
    h_spec = pl.BlockSpec((1, hg, s, HEAD_DIM), lambda bi, hi: (bi, hi, 0, 0))
    in_specs = [h_spec] * (len(qs) + 2)
    scratch = [pltpu.VMEM((ways, nslots, BLK, BLK), jnp.float32),
               pltpu.VMEM((ways, nslots * BLK, BLK), jnp.bfloat16),
               pltpu.VMEM((ways, HEAD_DIM, s), jnp.bfloat16)]
    if kind == MOBA:
        scratch.append(pltpu.VMEM((ways, nblk, nblk, BLK), jnp.float32))
    elif kind == FOX:
        in_specs.append(pl.BlockSpec((1, hg, 1, s), lambda bi, hi: (bi, hi, 0, 0)))
        scratch.append(pltpu.VMEM((ways, nblk, BLK, BLK), jnp.float32))
    else:
        in_specs += [pl.BlockSpec((4, DIFF_SUB_DIM), lambda bi, hi: (0, 0)),
                     pl.BlockSpec((1, HEAD_DIM), lambda bi, hi: (0, 0))]
    return pl.pallas_call(
        functools.partial(_attn_kernel, kind, nblk, hg, lambda_init),
        grid=(b, h // hg), in_specs=in_specs,
        out_specs=h_spec,
        out_shape=jax.ShapeDtypeStruct((b, h, s, HEAD_DIM), jnp.bfloat16),
        scratch_shapes=scratch,
        compiler_params=pltpu.CompilerParams(
            dimension_semantics=("arbitrary", "arbitrary"), vmem_limit_bytes=VMEM_LIMIT_BYTES),
        name=f"attn_{KIND_NAMES[kind]}",
    )(*qs, k, v, *extras)


def _gated(y_ref, sz_ref):
    y = jnp.concatenate([y_ref[0, h] for h in range(y_ref.shape[1])], axis=1)
    return (y.astype(jnp.float32) * sz_ref[0].astype(jnp.float32)).astype(jnp.bfloat16)


def _out_kernel(y_ref, sz_ref, x_ref, w_ref, o_ref):
    o_ref[0] = x_ref[0] + _dot(_gated(y_ref, sz_ref), w_ref[...])


def _out_project(y, sz, x, w, tm=1024):
    b, s, d = x.shape
    h = y.shape[1]
    width = h * HEAD_DIM
    row_spec = pl.BlockSpec((1, tm, d), lambda bi, si: (bi, si, 0))
    return pl.pallas_call(
        _out_kernel, grid=(b, s // tm),
        in_specs=[pl.BlockSpec((1, h, tm, HEAD_DIM), lambda bi, si: (bi, 0, si, 0)),
                  pl.BlockSpec((1, tm, width), lambda bi, si: (bi, si, 0)),
                  row_spec, pl.BlockSpec((width, d), lambda bi, si: (0, 0))],
        out_specs=row_spec,
        out_shape=jax.ShapeDtypeStruct((b, s, d), jnp.float32),
        compiler_params=pltpu.CompilerParams(
            dimension_semantics=("arbitrary", "arbitrary"), vmem_limit_bytes=VMEM_LIMIT_BYTES),
        name="out_proj",
    )(y, sz, x, w)


def _rope_tables(seq, dim):
    inv = 1.0 / (ROPE_THETA ** (jnp.arange(0, dim, 2, dtype=jnp.float32) / dim))
    ang = jnp.arange(seq, dtype=jnp.float32)[:, None] * inv[None, :]
    return jnp.cos(ang), jnp.sin(ang)


def kernel(x, norm_g, w_out, a_w_in, a_q_norm, a_k_norm, b_w_in, b_f_bias, b_q_norm, b_k_norm,
           c_w_in, c_q_norm, c_k_norm, c_lambda_q1, c_lambda_k1, c_lambda_q2, c_lambda_k2, c_subln):
    b, s, d = x.shape
    depth = norm_g.shape[0]
    width = N_HEADS * HEAD_DIM
    assert s % BLK == 0 and BLK == MOBA_BLOCK and d == width

    cos_h, sin_h = _rope_tables(s, HEAD_DIM)
    moba_tabs = (jnp.concatenate([cos_h, cos_h], axis=1), jnp.concatenate([-sin_h, sin_h], axis=1))
    cos_d, sin_d = _rope_tables(s, DIFF_SUB_DIM)
    zero = jnp.zeros_like(sin_d)
    diff_tabs = (jnp.concatenate([cos_d] * 4, axis=1),
                 jnp.concatenate([-sin_d, zero, -sin_d, zero], axis=1),
                 jnp.concatenate([zero, sin_d, zero, sin_d], axis=1))

    prev = None
    for i in range(depth):
        kind, j = i % 3, i // 3
        ng = norm_g[i][None, :]
        if kind == MOBA:
            w = a_w_in[j].astype(jnp.bfloat16)
            outs = _project(MOBA, x, ng, w, a_q_norm[j][None, :], a_k_norm[j][None, :], moba_tabs, prev)
        elif kind == FOX:
            w = jnp.pad(b_w_in[j], ((0, 0), (0, LANES - N_HEADS))).astype(jnp.bfloat16)
            fb = jnp.pad(b_f_bias[j], (0, LANES - N_HEADS))[None, :]
            outs = _project(FOX, x, ng, w, b_q_norm[j][None, :], b_k_norm[j][None, :], (fb,), prev)
        else:
            w = c_w_in[j].astype(jnp.bfloat16)
            qg = jnp.concatenate([c_q_norm[j]] * 2)[None, :]
            kg = jnp.concatenate([c_k_norm[j]] * 2)[None, :]
            outs = _project(DIFF, x, ng, w, qg, kg, diff_tabs, prev)
        if prev is not None:
            x, *outs = outs
        if kind == MOBA:
            q, k, v, sz = outs
            y = _attention(MOBA, (q,), k, v, ())
        elif kind == FOX:
            q, k, v, sz, c = outs
            y = _attention(FOX, (q,), k, v, (c,))
        else:
            qa, qb, k, v, sz = outs
            lam = jnp.stack([c_lambda_q1[j], c_lambda_k1[j], c_lambda_q2[j], c_lambda_k2[j]])
            lambda_init = 0.8 - 0.6 * math.exp(-0.3 * i)
            y = _attention(DIFF, (qa, qb), k, v, (lam, c_subln[j][None, :]), lambda_init)
        prev = (y, sz, w_out[i].astype(jnp.bfloat16))
    return _out_project(*prev[:2], x, prev[2])
```

```python
import functools
import math

import jax
import jax.numpy as jnp
from jax import lax
from jax.experimental import pallas as pl
from jax.experimental.pallas import tpu as pltpu

N_HEADS = 8
HEAD_DIM = 128
ROPE_THETA = 10000.0
NORM_EPS = 1e-6
MOBA_BLOCK = 256
MOBA_TOPK = 3
DIFF_SUB_DIM = HEAD_DIM // 2

BLK = 256
QK_LOOKAHEAD = (3, 3, 2)
ATTN_HEADS_PER_STEP = 4
HEADS_INTERLEAVED = (1, 2, 1)
LONG_TILES_FIRST = (False, False, True)
FIRST_LAYER_TM = 1024
LANES = 128
LOG2E = 1.4426950408889634
NEG_INF = float("-inf")
VMEM_LIMIT_BYTES = 56 * 1024 * 1024

MOBA, FOX, DIFF = 0, 1, 2
KIND_NAMES = ("moba", "fox", "diff")


def _dot(a, b):
    return jnp.dot(a, b, preferred_element_type=jnp.float32)


def _dot_nt(a, b):
    return lax.dot_general(a, b, (((1,), (1,)), ((), ())), preferred_element_type=jnp.float32)


def _head_norm_rope(kind, xh, ssq, gain, tabs, scale):
    norm_dim = DIFF_SUB_DIM if kind == DIFF else HEAD_DIM
    y = xh * lax.rsqrt(ssq * (1.0 / norm_dim) + NORM_EPS) * gain
    if kind == MOBA:
        cos_t, sin_t = tabs
        y = y * cos_t + pltpu.roll(y, HEAD_DIM // 2, axis=1) * sin_t
    elif kind == DIFF:
        cos_t, s_up, s_dn = tabs
        y = (y * cos_t + pltpu.roll(y, HEAD_DIM - DIFF_SUB_DIM // 2, axis=1) * s_up
             + pltpu.roll(y, DIFF_SUB_DIM // 2, axis=1) * s_dn)
    if scale != 1.0:
        y = y * scale
    return y


def _proj_kernel(kind, fused, tm, width, *refs):
    if fused:
        y_ref, szp_ref, wout_ref = refs[:3]
        refs = refs[3:]
    if kind == MOBA:
        (x_ref, ng_ref, w_ref, qg_ref, kg_ref, cos_ref, sin_ref, *outs, hdn_ref) = refs
        tabs = (cos_ref[...], sin_ref[...])
    elif kind == FOX:
        (x_ref, ng_ref, w_ref, qg_ref, kg_ref, fb_ref, *outs, hdn_ref, carry_ref) = refs
        tabs = None
    else:
        (x_ref, ng_ref, w_ref, qg_ref, kg_ref, cos_ref, sup_ref, sdn_ref, *outs, hdn_ref) = refs
        tabs = (cos_ref[...], sup_ref[...], sdn_ref[...])
    if fused:
        xnew_ref, *outs = outs
    if kind == MOBA:
        q_ref, k_ref, v_ref, sz_ref = outs
    elif kind == FOX:
        q_ref, k_ref, v_ref, sz_ref, c_ref = outs
    else:
        q_ref, qb_ref, k_ref, v_ref, sz_ref = outs

    if kind == FOX:
        @pl.when(pl.program_id(1) == 0)
        def _():
            carry_ref[...] = jnp.zeros_like(carry_ref)

    x = x_ref[0]
    if fused:
        x = x + _dot(_gated(y_ref, szp_ref), wout_ref[...])
        xnew_ref[0] = x
    ms = jnp.mean(x * x, axis=-1, keepdims=True)
    hdn_ref[...] = (x * lax.rsqrt(ms + NORM_EPS) * ng_ref[...]).astype(jnp.bfloat16)

    norm_dim = DIFF_SUB_DIM if kind == DIFF else HEAD_DIM
    q_scale = (norm_dim ** -0.5) * LOG2E
    pair = 2 * HEAD_DIM
    shift = norm_dim.bit_length() - 1
    gi = lax.broadcasted_iota(jnp.int32, (pair, pair), 0) >> shift
    gj = lax.broadcasted_iota(jnp.int32, (pair, pair), 1) >> shift
    group_ones = jnp.where(gi == gj, 1.0, 0.0).astype(jnp.bfloat16)

    def pair_dot(g, p):
        return _dot(hdn_ref[...], w_ref[:, g * width + p * pair:g * width + (p + 1) * pair])

    def v_store(p, rv):
        for half in range(2):
            v_ref[0, 2 * p + half] = rv[:, half * HEAD_DIM:(half + 1) * HEAD_DIM].astype(jnp.bfloat16)

    def qk_epilogue(p, rq, rk):
        if kind != FOX:
            ssq_q = _dot((rq * rq).astype(jnp.bfloat16), group_ones)
            ssq_k = _dot((rk * rk).astype(jnp.bfloat16), group_ones)
        for half in range(2):
            h = 2 * p + half
            sl = slice(half * HEAD_DIM, (half + 1) * HEAD_DIM)
            if kind == FOX:
                sq, sk = (jnp.sum(r[:, sl] * r[:, sl], axis=-1, keepdims=True) for r in (rq, rk))
            else:
                sq, sk = ssq_q[:, sl], ssq_k[:, sl]
            q = _head_norm_rope(kind, rq[:, sl], sq, qg_ref[...], tabs, q_scale)
            kk = _head_norm_rope(kind, rk[:, sl], sk, kg_ref[...], tabs, 1.0)
            if kind == DIFF:
                lane = lax.broadcasted_iota(jnp.int32, q.shape, 1)
                lo = lane < DIFF_SUB_DIM
                q_ref[0, h] = jnp.where(lo, q, 0.0).astype(jnp.bfloat16)
                qb_ref[0, h] = jnp.where(lo, 0.0, q).astype(jnp.bfloat16)
            else:
                q_ref[0, h] = q.astype(jnp.bfloat16)
            k_ref[0, h] = kk.astype(jnp.bfloat16)

    def forget_gates(fl):
        head_lanes = lax.broadcasted_iota(jnp.int32, fl.shape, 1) < N_HEADS
        logf = jnp.minimum(fl, 0.0) - jnp.log(1.0 + jnp.exp(-jnp.abs(fl)))
        logf = jnp.where(head_lanes, logf, 0.0)
        row = lax.broadcasted_iota(jnp.int32, (tm, tm), 0)
        col = lax.broadcasted_iota(jnp.int32, (tm, tm), 1)
        tril = jnp.where(row >= col, 1.0, 0.0).astype(jnp.bfloat16)
        hi = logf.astype(jnp.bfloat16).astype(jnp.float32)
        r1 = logf - hi
        mid = r1.astype(jnp.bfloat16).astype(jnp.float32)
        lo_ = (r1 - mid).astype(jnp.bfloat16).astype(jnp.float32)
        parts = hi + pltpu.roll(mid, N_HEADS, axis=1) + pltpu.roll(lo_, 2 * N_HEADS, axis=1)
        cs = _dot(tril, parts.astype(jnp.bfloat16))
        csum = cs + pltpu.roll(cs, LANES - N_HEADS, axis=1) + pltpu.roll(cs, LANES - 2 * N_HEADS, axis=1)
        csum = jnp.where(head_lanes, csum, 0.0)
        c_tile = csum + carry_ref[0:1, :]
        carry_ref[0:1, :] = c_tile[tm - 1:tm, :]
        cT = (c_tile * LOG2E).T
        for h in range(N_HEADS):
            c_ref[0, h] = cT[h:h + 1, :]

    if kind == FOX:
        fl = _dot(hdn_ref[...], w_ref[:, 4 * width:4 * width + LANES]) + fb_ref[...]
    n_pairs = N_HEADS // 2
    prev = None
    for p in range(n_pairs):
        cur = (pair_dot(0, p), pair_dot(1, p))
        v_store(p, pair_dot(2, p))
        if kind == FOX and p == 0:
            forget_gates(fl)
        if prev is not None:
            qk_epilogue(p - 1, *prev)
        prev = cur
    zc = width // 2
    rz = [None, None]
    rz[0] = _dot(hdn_ref[...], w_ref[:, 3 * width:3 * width + zc])
    qk_epilogue(n_pairs - 1, *prev)
    rz[1] = _dot(hdn_ref[...], w_ref[:, 3 * width + zc:4 * width])
    for c in range(2):
        sz_ref[0, :, c * zc:(c + 1) * zc] = (rz[c] * (1.0 / (1.0 + jnp.exp(-rz[c])))).astype(jnp.bfloat16)


def _project(kind, x, ng, w, qg, kg, extras, prev=None, tm=512):
    b, s, d = x.shape
    width = N_HEADS * HEAD_DIM
    grid = (b, s // tm)
    fused = prev is not None

    def const(shape):
        return pl.BlockSpec(shape, lambda bi, si: (0,) * len(shape))

    tab_spec = pl.BlockSpec((tm, HEAD_DIM), lambda bi, si: (si, 0))
    x_spec = pl.BlockSpec((1, tm, d), lambda bi, si: (bi, si, 0))
    h_spec = pl.BlockSpec((1, N_HEADS, tm, HEAD_DIM), lambda bi, si: (bi, 0, si, 0))
    sz_spec = pl.BlockSpec((1, tm, width), lambda bi, si: (bi, si, 0))
    in_specs = [h_spec, sz_spec, const((width, d))] if fused else []
    in_specs += [x_spec, const((1, d)), const(w.shape), const((1, HEAD_DIM)), const((1, HEAD_DIM))]
    if kind == MOBA:
        in_specs += [tab_spec, tab_spec]
    elif kind == FOX:
        in_specs += [const((1, LANES))]
    else:
        in_specs += [tab_spec, tab_spec, tab_spec]

    h_shape = jax.ShapeDtypeStruct((b, N_HEADS, s, HEAD_DIM), jnp.bfloat16)
    sz_shape = jax.ShapeDtypeStruct((b, s, width), jnp.bfloat16)

    n_heads_out = 4 if kind == DIFF else 3
    out_shape = [h_shape] * n_heads_out + [sz_shape]
    out_specs = [h_spec] * n_heads_out + [sz_spec]
    if fused:
        out_shape.insert(0, jax.ShapeDtypeStruct((b, s, d), jnp.float32))
        out_specs.insert(0, x_spec)
    scratch = [pltpu.VMEM((tm, d), jnp.bfloat16)]
    if kind == FOX:
        out_shape.append(jax.ShapeDtypeStruct((b, N_HEADS, 1, s), jnp.float32))
        out_specs.append(pl.BlockSpec((1, N_HEADS, 1, tm), lambda bi, si: (bi, 0, 0, si)))
        scratch.append(pltpu.VMEM((8, LANES), jnp.float32))

    return pl.pallas_call(
        functools.partial(_proj_kernel, kind, fused, tm, width),
        grid=grid, in_specs=in_specs, out_specs=out_specs, out_shape=out_shape,
        scratch_shapes=scratch,
        compiler_params=pltpu.CompilerParams(
            dimension_semantics=("arbitrary", "arbitrary"), vmem_limit_bytes=VMEM_LIMIT_BYTES),
        name=f"proj_{KIND_NAMES[kind]}",
    )(*(prev or ()), x, ng, w, qg, kg, *extras)


def _attn_kernel(kind, nblk, heads, lambda_init, *refs):
    zero_ref, *refs = refs
    if kind == MOBA:
        q_ref, k_ref, v_ref, o_ref, s_ref, p_ref, vT_ref, bias_ref = refs
        q_refs = (q_ref,)
    elif kind == FOX:
        q_ref, k_ref, v_ref, c_ref, o_ref, s_ref, p_ref, vT_ref, ck_ref = refs
        q_refs = (q_ref,)
    else:
        q_ref, qb_ref, k_ref, v_ref, lam_ref, sg_ref, o_ref, s_ref, p_ref, vT_ref = refs
        q_refs = (q_ref, qb_ref)
    ways = HEADS_INTERLEAVED[kind]
    refs = (*refs, zero_ref[0])

    def group(g, carry):
        stages = [_attn_head_stages(kind, nblk, lambda_init, refs, q_refs, w, g * ways + w) for w in range(ways)]
        pending = [{} for _ in range(ways)]
        lookahead = QK_LOOKAHEAD[kind]
        order = list(range(nblk - 1, -1, -1)) if LONG_TILES_FIRST[kind] else list(range(nblk))
        for step in range(nblk + lookahead):
            for w, (scores, finish) in enumerate(stages):
                if step < nblk:
                    pending[w][order[step]] = scores(order[step])
            done = step - lookahead
            for w, (scores, finish) in enumerate(stages):
                if done >= 0:
                    finish(order[done], pending[w].pop(order[done]))
        return carry

    lax.fori_loop(0, heads // ways, group, 0)


def _attn_head_stages(kind, nblk, lambda_init, refs, q_refs, w, hh):
    *refs, zero = refs
    if kind == MOBA:
        _, k_ref, v_ref, o_ref, s_ref, p_ref, vT_ref, bias_ref = refs
        bias_ref = bias_ref.at[w]
    elif kind == FOX:
        _, k_ref, v_ref, c_ref, o_ref, s_ref, p_ref, vT_ref, ck_ref = refs
        ck_ref = ck_ref.at[w]
        cv = c_ref.at[0, hh]
    else:
        _, _, k_ref, v_ref, lam_ref, sg_ref, o_ref, s_ref, p_ref, vT_ref = refs
    s_ref, p_ref, vT_ref = s_ref.at[w], p_ref.at[w], vT_ref.at[w]
    nvar = len(q_refs)
    qv = [r.at[0, hh] for r in q_refs]
    kv = k_ref.at[0, hh]
    ov = o_ref.at[0, hh]

    vT_ref[...] = v_ref[0, hh].T

    if kind == MOBA:
        kmean = jnp.concatenate(
            [jnp.sum(kv[j * BLK:(j + 1) * BLK, :].astype(jnp.float32), axis=0, keepdims=True)
             for j in range(nblk)], axis=0)
        kmean = (kmean * (1.0 / MOBA_BLOCK)).astype(jnp.bfloat16)
        row = lax.broadcasted_iota(jnp.int32, (nblk, BLK), 0)
        for t in range(1, nblk):
            gate = _dot_nt(kmean, qv[0][t * BLK:(t + 1) * BLK, :])
            past = row < t
            gm = jnp.where(past, gate, NEG_INF)
            cnt = jnp.zeros((nblk, BLK), jnp.int32)
            for r in range(1, nblk):
                other = pltpu.roll(gm, r, axis=0)
                tie_first = jnp.where(row >= r, 1, 0)
                cnt = cnt + jnp.where(other > gm, 1, jnp.where(other == gm, tie_first, 0))
            bias_ref[t] = jnp.where(past, jnp.where(cnt < MOBA_TOPK, 0.0, NEG_INF), NEG_INF)
    elif kind == FOX:
        for j in range(nblk):
            ck_ref[j] = jnp.broadcast_to(cv[:, j * BLK:(j + 1) * BLK], (BLK, BLK)).T

    krow = lax.broadcasted_iota(jnp.int32, (BLK, BLK), 0)
    qcol = lax.broadcasted_iota(jnp.int32, (BLK, BLK), 1)
    causal = krow <= qcol

    def base_slot(qi, v):
        return nvar * qi * (qi + 1) // 2 + v * (qi + 1)

    def scores(qi):
        nk = qi + 1
        q_sl = slice(qi * BLK, (qi + 1) * BLK)
        k_all = kv[0:nk * BLK, :]
        shifts = []
        for v in range(nvar):
            base = base_slot(qi, v)
            s = _dot_nt(k_all, qv[v][q_sl, :])
            m = None
            for j in range(nk):
                sj = s[j * BLK:(j + 1) * BLK]
                if kind == FOX:
                    sj = sj - ck_ref[j]
                if j == qi:
                    sj = jnp.where(causal, sj, NEG_INF)
                s_ref[base + j + zero] = sj
                cm = jnp.max(sj, axis=0, keepdims=True)
                if kind == MOBA and j < qi:
                    cm = cm + bias_ref[qi, j:j + 1, :]
                m = cm if m is None else jnp.maximum(m, cm)
            if kind == FOX:
                cq = cv[:, q_sl]
                shifts.append(cq - (m + cq))
            else:
                shifts.append(-m)
        return shifts

    def finish(qi, shifts):
        nk = qi + 1
        q_sl = slice(qi * BLK, (qi + 1) * BLK)
        v_all = vT_ref[:, 0:nk * BLK]
        outs = []
        for v in range(nvar):
            base = base_slot(qi, v)
            l = None
            for j in range(nk):
                sh = shifts[v] + bias_ref[qi, j:j + 1, :] if (kind == MOBA and j < qi) else shifts[v]
                pj = jnp.exp2(s_ref[base + j + zero] + sh)
                lj = jnp.sum(pj, axis=0, keepdims=True)
                l = lj if l is None else l + lj
                p_ref[(base + j) * BLK:(base + j + 1) * BLK, :] = pj.astype(jnp.bfloat16)
            acc = _dot(v_all, p_ref[base * BLK:(base + nk) * BLK, :])
            outs.append(acc * (1.0 / l))
        if kind == DIFF:
            lam = (jnp.exp(jnp.sum(lam_ref[0:1, :] * lam_ref[1:2, :], axis=1, keepdims=True))
                   - jnp.exp(jnp.sum(lam_ref[2:3, :] * lam_ref[3:4, :], axis=1, keepdims=True))
                   + lambda_init)
            y = (outs[0] - lam * outs[1]).T
            msq = jnp.mean(y * y, axis=-1, keepdims=True)
            y = y * lax.rsqrt(msq + NORM_EPS) * sg_ref[...] * (1.0 - lambda_init)
        else:
            y = outs[0].T
        ov[q_sl, :] = y.astype(jnp.bfloat16)

    return scores, finish


def _attention(kind, qs, k, v, extras, lambda_init=0.0):
    b, h, s, _ = k.shape
    nblk = s // BLK
    hg = ATTN_HEADS_PER_STEP
    ways = HEADS_INTERLEAVED[kind]
    nslots = len(qs) * nblk * (nblk + 1) // 2
    h_spec = pl.BlockSpec((1, hg, s, HEAD_DIM), lambda bi, hi: (bi, hi, 0, 0))
    in_specs = [pl.BlockSpec(memory_space=pltpu.SMEM)] + [h_spec] * (len(qs) + 2)
    scratch = [pltpu.VMEM((ways, nslots, BLK, BLK), jnp.float32),
               pltpu.VMEM((ways, nslots * BLK, BLK), jnp.bfloat16),
               pltpu.VMEM((ways, HEAD_DIM, s), jnp.bfloat16)]
    if kind == MOBA:
        scratch.append(pltpu.VMEM((ways, nblk, nblk, BLK), jnp.float32))
    elif kind == FOX:
        in_specs.append(pl.BlockSpec((1, hg, 1, s), lambda bi, hi: (bi, hi, 0, 0)))
        scratch.append(pltpu.VMEM((ways, nblk, BLK, BLK), jnp.float32))
    else:
        in_specs += [pl.BlockSpec((4, DIFF_SUB_DIM), lambda bi, hi: (0, 0)),
                     pl.BlockSpec((1, HEAD_DIM), lambda bi, hi: (0, 0))]
    return pl.pallas_call(
        functools.partial(_attn_kernel, kind, nblk, hg, lambda_init),
        grid=(b, h // hg), in_specs=in_specs,
        out_specs=h_spec,
        out_shape=jax.ShapeDtypeStruct((b, h, s, HEAD_DIM), jnp.bfloat16),
        scratch_shapes=scratch,
        compiler_params=pltpu.CompilerParams(
            dimension_semantics=("arbitrary", "arbitrary"), vmem_limit_bytes=VMEM_LIMIT_BYTES),
        name=f"attn_{KIND_NAMES[kind]}",
    )(jnp.zeros((1,), jnp.int32), *qs, k, v, *extras)


def _gated(y_ref, sz_ref):
    y = jnp.concatenate([y_ref[0, h] for h in range(y_ref.shape[1])], axis=1)
    return (y.astype(jnp.float32) * sz_ref[0].astype(jnp.float32)).astype(jnp.bfloat16)


def _out_kernel(y_ref, sz_ref, x_ref, w_ref, o_ref):
    o_ref[0] = x_ref[0] + _dot(_gated(y_ref, sz_ref), w_ref[...])


def _out_project(y, sz, x, w, tm=1024):
    b, s, d = x.shape
    h = y.shape[1]
    width = h * HEAD_DIM
    row_spec = pl.BlockSpec((1, tm, d), lambda bi, si: (bi, si, 0))
    return pl.pallas_call(
        _out_kernel, grid=(b, s // tm),
        in_specs=[pl.BlockSpec((1, h, tm, HEAD_DIM), lambda bi, si: (bi, 0, si, 0)),
                  pl.BlockSpec((1, tm, width), lambda bi, si: (bi, si, 0)),
                  row_spec, pl.BlockSpec((width, d), lambda bi, si: (0, 0))],
        out_specs=row_spec,
        out_shape=jax.ShapeDtypeStruct((b, s, d), jnp.float32),
        compiler_params=pltpu.CompilerParams(
            dimension_semantics=("arbitrary", "arbitrary"), vmem_limit_bytes=VMEM_LIMIT_BYTES),
        name="out_proj",
    )(y, sz, x, w)


def _rope_tables(seq, dim):
    inv = 1.0 / (ROPE_THETA ** (jnp.arange(0, dim, 2, dtype=jnp.float32) / dim))
    ang = jnp.arange(seq, dtype=jnp.float32)[:, None] * inv[None, :]
    return jnp.cos(ang), jnp.sin(ang)


def kernel(x, norm_g, w_out, a_w_in, a_q_norm, a_k_norm, b_w_in, b_f_bias, b_q_norm, b_k_norm,
           c_w_in, c_q_norm, c_k_norm, c_lambda_q1, c_lambda_k1, c_lambda_q2, c_lambda_k2, c_subln):
    b, s, d = x.shape
    depth = norm_g.shape[0]
    width = N_HEADS * HEAD_DIM
    assert s % BLK == 0 and BLK == MOBA_BLOCK and d == width

    cos_h, sin_h = _rope_tables(s, HEAD_DIM)
    moba_tabs = (jnp.concatenate([cos_h, cos_h], axis=1), jnp.concatenate([-sin_h, sin_h], axis=1))
    cos_d, sin_d = _rope_tables(s, DIFF_SUB_DIM)
    zero = jnp.zeros_like(sin_d)
    diff_tabs = (jnp.concatenate([cos_d] * 4, axis=1),
                 jnp.concatenate([-sin_d, zero, -sin_d, zero], axis=1),
                 jnp.concatenate([zero, sin_d, zero, sin_d], axis=1))

    prev = None
    for i in range(depth):
        kind, j = i % 3, i // 3
        ng = norm_g[i][None, :]
        tm = 512 if prev is not None else FIRST_LAYER_TM
        if kind == MOBA:
            w = a_w_in[j].astype(jnp.bfloat16)
            outs = _project(MOBA, x, ng, w, a_q_norm[j][None, :], a_k_norm[j][None, :], moba_tabs, prev, tm)
        elif kind == FOX:
            w = jnp.pad(b_w_in[j], ((0, 0), (0, LANES - N_HEADS))).astype(jnp.bfloat16)
            fb = jnp.pad(b_f_bias[j], (0, LANES - N_HEADS))[None, :]
            outs = _project(FOX, x, ng, w, b_q_norm[j][None, :], b_k_norm[j][None, :], (fb,), prev, tm)
        else:
            w = c_w_in[j].astype(jnp.bfloat16)
            qg = jnp.concatenate([c_q_norm[j]] * 2)[None, :]
            kg = jnp.concatenate([c_k_norm[j]] * 2)[None, :]
            outs = _project(DIFF, x, ng, w, qg, kg, diff_tabs, prev, tm)
        if prev is not None:
            x, *outs = outs
        if kind == MOBA:
            q, k, v, sz = outs
            y = _attention(MOBA, (q,), k, v, ())
        elif kind == FOX:
            q, k, v, sz, c = outs
            y = _attention(FOX, (q,), k, v, (c,))
        else:
            qa, qb, k, v, sz = outs
            lam = jnp.stack([c_lambda_q1[j], c_lambda_k1[j], c_lambda_q2[j], c_lambda_k2[j]])
            lambda_init = 0.8 - 0.6 * math.exp(-0.3 * i)
            y = _attention(DIFF, (qa, qb), k, v, (lam, c_subln[j][None, :]), lambda_init)
        prev = (y, sz, w_out[i].astype(jnp.bfloat16))
    return _out_project(*prev[:2], x, prev[2])
```

```python
import functools
import math

import jax
import jax.numpy as jnp
from jax import lax
from jax.experimental import pallas as pl
from jax.experimental.pallas import tpu as pltpu

N_HEADS = 8
HEAD_DIM = 128
ROPE_THETA = 10000.0
NORM_EPS = 1e-6
MOBA_BLOCK = 256
MOBA_TOPK = 3
DIFF_SUB_DIM = HEAD_DIM // 2

BLK = 256
QK_LOOKAHEAD = (4, 4, 2)
ATTN_HEADS_PER_STEP = 4
HEADS_INTERLEAVED = (1, 2, 1)
LONG_TILES_FIRST = (False, False, True)
GATE_ROWS = 16
FIRST_LAYER_TM = 1024
LANES = 128
LOG2E = 1.4426950408889634
NEG_INF = float("-inf")
VMEM_LIMIT_BYTES = 56 * 1024 * 1024

MOBA, FOX, DIFF = 0, 1, 2
KIND_NAMES = ("moba", "fox", "diff")


def _dot(a, b):
    return jnp.dot(a, b, preferred_element_type=jnp.float32)


def _dot_nt(a, b):
    return lax.dot_general(a, b, (((1,), (1,)), ((), ())), preferred_element_type=jnp.float32)


def _head_norm_rope(kind, xh, ssq, gain, tabs, scale):
    norm_dim = DIFF_SUB_DIM if kind == DIFF else HEAD_DIM
    y = xh * lax.rsqrt(ssq * (1.0 / norm_dim) + NORM_EPS) * gain
    if kind == MOBA:
        cos_t, sin_t = tabs
        y = y * cos_t + pltpu.roll(y, HEAD_DIM // 2, axis=1) * sin_t
    elif kind == DIFF:
        cos_t, s_up, s_dn = tabs
        y = (y * cos_t + pltpu.roll(y, HEAD_DIM - DIFF_SUB_DIM // 2, axis=1) * s_up
             + pltpu.roll(y, DIFF_SUB_DIM // 2, axis=1) * s_dn)
    if scale != 1.0:
        y = y * scale
    return y


def _proj_kernel(kind, fused, tm, width, *refs):
    if fused:
        y_ref, szp_ref, wout_ref = refs[:3]
        refs = refs[3:]
    if kind == MOBA:
        (x_ref, ng_ref, w_ref, qg_ref, kg_ref, cos_ref, sin_ref, *outs, hdn_ref) = refs
        tabs = (cos_ref[...], sin_ref[...])
    elif kind == FOX:
        (x_ref, ng_ref, w_ref, qg_ref, kg_ref, fb_ref, *outs, hdn_ref, carry_ref) = refs
        tabs = None
    else:
        (x_ref, ng_ref, w_ref, qg_ref, kg_ref, cos_ref, sup_ref, sdn_ref, *outs, hdn_ref) = refs
        tabs = (cos_ref[...], sup_ref[...], sdn_ref[...])
    if fused:
        xnew_ref, *outs = outs
    if kind == MOBA:
        q_ref, k_ref, v_ref, sz_ref = outs
    elif kind == FOX:
        q_ref, k_ref, v_ref, sz_ref, c_ref = outs
    else:
        q_ref, qb_ref, k_ref, v_ref, sz_ref = outs

    if kind == FOX:
        @pl.when(pl.program_id(1) == 0)
        def _():
            carry_ref[...] = jnp.zeros_like(carry_ref)

    x = x_ref[0]
    if fused:
        x = x + _dot(_gated(y_ref, szp_ref), wout_ref[...])
        xnew_ref[0] = x
    ms = jnp.mean(x * x, axis=-1, keepdims=True)
    hdn_ref[...] = (x * lax.rsqrt(ms + NORM_EPS) * ng_ref[...]).astype(jnp.bfloat16)

    norm_dim = DIFF_SUB_DIM if kind == DIFF else HEAD_DIM
    q_scale = (norm_dim ** -0.5) * LOG2E
    pair = 2 * HEAD_DIM
    shift = norm_dim.bit_length() - 1
    gi = lax.broadcasted_iota(jnp.int32, (pair, pair), 0) >> shift
    gj = lax.broadcasted_iota(jnp.int32, (pair, pair), 1) >> shift
    group_ones = jnp.where(gi == gj, 1.0, 0.0).astype(jnp.bfloat16)

    def pair_dot(g, p):
        return _dot(hdn_ref[...], w_ref[:, g * width + p * pair:g * width + (p + 1) * pair])

    def v_store(p, rv):
        for half in range(2):
            v_ref[0, 2 * p + half] = rv[:, half * HEAD_DIM:(half + 1) * HEAD_DIM].astype(jnp.bfloat16)

    def qk_epilogue(p, rq, rk):
        if kind != FOX:
            ssq_q = _dot((rq * rq).astype(jnp.bfloat16), group_ones)
            ssq_k = _dot((rk * rk).astype(jnp.bfloat16), group_ones)
        for half in range(2):
            h = 2 * p + half
            sl = slice(half * HEAD_DIM, (half + 1) * HEAD_DIM)
            if kind == FOX:
                sq, sk = (jnp.sum(r[:, sl] * r[:, sl], axis=-1, keepdims=True) for r in (rq, rk))
            else:
                sq, sk = ssq_q[:, sl], ssq_k[:, sl]
            q = _head_norm_rope(kind, rq[:, sl], sq, qg_ref[...], tabs, q_scale)
            kk = _head_norm_rope(kind, rk[:, sl], sk, kg_ref[...], tabs, 1.0)
            if kind == DIFF:
                lane = lax.broadcasted_iota(jnp.int32, q.shape, 1)
                lo = lane < DIFF_SUB_DIM
                q_ref[0, h] = jnp.where(lo, q, 0.0).astype(jnp.bfloat16)
                qb_ref[0, h] = jnp.where(lo, 0.0, q).astype(jnp.bfloat16)
            else:
                q_ref[0, h] = q.astype(jnp.bfloat16)
            k_ref[0, h] = kk.astype(jnp.bfloat16)

    def forget_gates(fl):
        head_lanes = lax.broadcasted_iota(jnp.int32, fl.shape, 1) < N_HEADS
        logf = jnp.minimum(fl, 0.0) - jnp.log(1.0 + jnp.exp(-jnp.abs(fl)))
        logf = jnp.where(head_lanes, logf, 0.0)
        row = lax.broadcasted_iota(jnp.int32, (tm, tm), 0)
        col = lax.broadcasted_iota(jnp.int32, (tm, tm), 1)
        tril = jnp.where(row >= col, 1.0, 0.0).astype(jnp.bfloat16)
        hi = logf.astype(jnp.bfloat16).astype(jnp.float32)
        r1 = logf - hi
        mid = r1.astype(jnp.bfloat16).astype(jnp.float32)
        lo_ = (r1 - mid).astype(jnp.bfloat16).astype(jnp.float32)
        parts = hi + pltpu.roll(mid, N_HEADS, axis=1) + pltpu.roll(lo_, 2 * N_HEADS, axis=1)
        cs = _dot(tril, parts.astype(jnp.bfloat16))
        csum = cs + pltpu.roll(cs, LANES - N_HEADS, axis=1) + pltpu.roll(cs, LANES - 2 * N_HEADS, axis=1)
        csum = jnp.where(head_lanes, csum, 0.0)
        c_tile = csum + carry_ref[0:1, :]
        carry_ref[0:1, :] = c_tile[tm - 1:tm, :]
        cT = (c_tile * LOG2E).T
        for h in range(N_HEADS):
            c_ref[0, h] = cT[h:h + 1, :]

    if kind == FOX:
        fl = _dot(hdn_ref[...], w_ref[:, 4 * width:4 * width + LANES]) + fb_ref[...]
    n_pairs = N_HEADS // 2
    prev = None
    for p in range(n_pairs):
        cur = (pair_dot(0, p), pair_dot(1, p))
        v_store(p, pair_dot(2, p))
        if kind == FOX and p == 0:
            forget_gates(fl)
        if prev is not None:
            qk_epilogue(p - 1, *prev)
        prev = cur
    zc = width // 2
    rz = [None, None]
    rz[0] = _dot(hdn_ref[...], w_ref[:, 3 * width:3 * width + zc])
    qk_epilogue(n_pairs - 1, *prev)
    rz[1] = _dot(hdn_ref[...], w_ref[:, 3 * width + zc:4 * width])
    for c in range(2):
        sz_ref[0, :, c * zc:(c + 1) * zc] = (rz[c] * (1.0 / (1.0 + jnp.exp(-rz[c])))).astype(jnp.bfloat16)


def _project(kind, x, ng, w, qg, kg, extras, prev=None, tm=512):
    b, s, d = x.shape
    width = N_HEADS * HEAD_DIM
    grid = (b, s // tm)
    fused = prev is not None

    def const(shape):
        return pl.BlockSpec(shape, lambda bi, si: (0,) * len(shape))

    tab_spec = pl.BlockSpec((tm, HEAD_DIM), lambda bi, si: (si, 0))
    x_spec = pl.BlockSpec((1, tm, d), lambda bi, si: (bi, si, 0))
    h_spec = pl.BlockSpec((1, N_HEADS, tm, HEAD_DIM), lambda bi, si: (bi, 0, si, 0))
    sz_spec = pl.BlockSpec((1, tm, width), lambda bi, si: (bi, si, 0))
    in_specs = [h_spec, sz_spec, const((width, d))] if fused else []
    in_specs += [x_spec, const((1, d)), const(w.shape), const((1, HEAD_DIM)), const((1, HEAD_DIM))]
    if kind == MOBA:
        in_specs += [tab_spec, tab_spec]
    elif kind == FOX:
        in_specs += [const((1, LANES))]
    else:
        in_specs += [tab_spec, tab_spec, tab_spec]

    h_shape = jax.ShapeDtypeStruct((b, N_HEADS, s, HEAD_DIM), jnp.bfloat16)
    sz_shape = jax.ShapeDtypeStruct((b, s, width), jnp.bfloat16)

    n_heads_out = 4 if kind == DIFF else 3
    out_shape = [h_shape] * n_heads_out + [sz_shape]
    out_specs = [h_spec] * n_heads_out + [sz_spec]
    if fused:
        out_shape.insert(0, jax.ShapeDtypeStruct((b, s, d), jnp.float32))
        out_specs.insert(0, x_spec)
    scratch = [pltpu.VMEM((tm, d), jnp.bfloat16)]
    if kind == FOX:
        out_shape.append(jax.ShapeDtypeStruct((b, N_HEADS, 1, s), jnp.float32))
        out_specs.append(pl.BlockSpec((1, N_HEADS, 1, tm), lambda bi, si: (bi, 0, 0, si)))
        scratch.append(pltpu.VMEM((8, LANES), jnp.float32))

    return pl.pallas_call(
        functools.partial(_proj_kernel, kind, fused, tm, width),
        grid=grid, in_specs=in_specs, out_specs=out_specs, out_shape=out_shape,
        scratch_shapes=scratch,
        compiler_params=pltpu.CompilerParams(
            dimension_semantics=("arbitrary", "arbitrary"), vmem_limit_bytes=VMEM_LIMIT_BYTES),
        name=f"proj_{KIND_NAMES[kind]}",
    )(*(prev or ()), x, ng, w, qg, kg, *extras)


def _attn_kernel(kind, nblk, heads, lambda_init, *refs):
    zero_ref, *refs = refs
    if kind == MOBA:
        q_ref, k_ref, v_ref, o_ref, s_ref, p_ref, vT_ref, bias_ref, kx_ref = refs
        q_refs = (q_ref,)
    elif kind == FOX:
        q_ref, k_ref, v_ref, c_ref, o_ref, s_ref, p_ref, vT_ref, ck_ref = refs
        q_refs = (q_ref,)
    else:
        q_ref, qb_ref, k_ref, v_ref, lam_ref, sg_ref, o_ref, s_ref, p_ref, vT_ref = refs
        q_refs = (q_ref, qb_ref)
    ways = HEADS_INTERLEAVED[kind]
    refs = (*refs, zero_ref[0])

    def group(g, carry):
        stages = [_attn_head_stages(kind, nblk, lambda_init, refs, q_refs, w, g * ways + w) for w in range(ways)]
        pending = [{} for _ in range(ways)]
        lookahead = QK_LOOKAHEAD[kind]
        order = list(range(nblk - 1, -1, -1)) if LONG_TILES_FIRST[kind] else list(range(nblk))
        for step in range(nblk + lookahead):
            for w, (scores, finish) in enumerate(stages):
                if step < nblk:
                    pending[w][order[step]] = scores(order[step])
            done = step - lookahead
            for w, (scores, finish) in enumerate(stages):
                if done >= 0:
                    finish(order[done], pending[w].pop(order[done]))
        return carry

    lax.fori_loop(0, heads // ways, group, 0)


def _attn_head_stages(kind, nblk, lambda_init, refs, q_refs, w, hh):
    *refs, zero = refs
    if kind == MOBA:
        _, k_ref, v_ref, o_ref, s_ref, p_ref, vT_ref, bias_ref, kx_ref = refs
        bias_ref, kx_ref = bias_ref.at[w], kx_ref.at[w]
    elif kind == FOX:
        _, k_ref, v_ref, c_ref, o_ref, s_ref, p_ref, vT_ref, ck_ref = refs
        ck_ref = ck_ref.at[w]
        cv = c_ref.at[0, hh]
    else:
        _, _, k_ref, v_ref, lam_ref, sg_ref, o_ref, s_ref, p_ref, vT_ref = refs
    s_ref, p_ref, vT_ref = s_ref.at[w], p_ref.at[w], vT_ref.at[w]
    nvar = len(q_refs)
    qv = [r.at[0, hh] for r in q_refs]
    kv = k_ref.at[0, hh]
    ov = o_ref.at[0, hh]

    vT_ref[...] = v_ref[0, hh].T

    if kind == MOBA:
        assert nblk <= GATE_ROWS
        kmean = jnp.concatenate(
            [jnp.sum(kv[j * BLK:(j + 1) * BLK, :].astype(jnp.float32), axis=0, keepdims=True)
             for j in range(nblk)] + [jnp.zeros((GATE_ROWS - nblk, HEAD_DIM), jnp.float32)], axis=0)
        kx_ref[0:GATE_ROWS, :] = (kmean * (1.0 / MOBA_BLOCK)).astype(jnp.bfloat16)
        kx_ref[GATE_ROWS:GATE_ROWS + nblk * BLK, :] = kv[...]
        kv = kx_ref
        key0 = GATE_ROWS
    else:
        key0 = 0
    if kind == FOX:
        for j in range(nblk):
            ck_ref[j] = jnp.broadcast_to(cv[:, j * BLK:(j + 1) * BLK], (BLK, BLK)).T

    krow = lax.broadcasted_iota(jnp.int32, (BLK, BLK), 0)
    qcol = lax.broadcasted_iota(jnp.int32, (BLK, BLK), 1)
    causal = krow <= qcol

    def base_slot(qi, v):
        return nvar * qi * (qi + 1) // 2 + v * (qi + 1)

    def scores(qi):
        nk = qi + 1
        q_sl = slice(qi * BLK, (qi + 1) * BLK)
        k_all = kv[0:key0 + nk * BLK, :]
        shifts = []
        for v in range(nvar):
            base = base_slot(qi, v)
            s = _dot_nt(k_all, qv[v][q_sl, :])
            if kind == MOBA and qi > 0:
                row = lax.broadcasted_iota(jnp.int32, (nblk, BLK), 0)
                past = row < qi
                gm = jnp.where(past, s[0:nblk], NEG_INF)
                cnt = jnp.zeros((nblk, BLK), jnp.int32)
                for r in range(1, nblk):
                    other = pltpu.roll(gm, r, axis=0)
                    tie_first = jnp.where(row >= r, 1, 0)
                    cnt = cnt + jnp.where(other > gm, 1, jnp.where(other == gm, tie_first, 0))
                bias_ref[qi] = jnp.where(past, jnp.where(cnt < MOBA_TOPK, 0.0, NEG_INF), NEG_INF)
            m = None
            for j in range(nk):
                sj = s[key0 + j * BLK:key0 + (j + 1) * BLK]
                if kind == FOX:
                    sj = sj - ck_ref[j]
                if j == qi:
                    sj = jnp.where(causal, sj, NEG_INF)
                s_ref[base + j + zero] = sj
                cm = jnp.max(sj, axis=0, keepdims=True)
                if kind == MOBA and j < qi:
                    cm = cm + bias_ref[qi, j:j + 1, :]
                m = cm if m is None else jnp.maximum(m, cm)
            if kind == FOX:
                cq = cv[:, q_sl]
                shifts.append(cq - (m + cq))
            else:
                shifts.append(-m)
        return shifts

    def finish(qi, shifts):
        nk = qi + 1
        q_sl = slice(qi * BLK, (qi + 1) * BLK)
        v_all = vT_ref[:, 0:nk * BLK]
        outs = []
        for v in range(nvar):
            base = base_slot(qi, v)
            l = None
            for j in range(nk):
                sh = shifts[v] + bias_ref[qi, j:j + 1, :] if (kind == MOBA and j < qi) else shifts[v]
                pj = jnp.exp2(s_ref[base + j + zero] + sh)
                lj = jnp.sum(pj, axis=0, keepdims=True)
                l = lj if l is None else l + lj
                p_ref[(base + j) * BLK:(base + j + 1) * BLK, :] = pj.astype(jnp.bfloat16)
            acc = _dot(v_all, p_ref[base * BLK:(base + nk) * BLK, :])
            outs.append(acc * (1.0 / l))
        if kind == DIFF:
            lam = (jnp.exp(jnp.sum(lam_ref[0:1, :] * lam_ref[1:2, :], axis=1, keepdims=True))
                   - jnp.exp(jnp.sum(lam_ref[2:3, :] * lam_ref[3:4, :], axis=1, keepdims=True))
                   + lambda_init)
            y = (outs[0] - lam * outs[1]).T
            msq = jnp.mean(y * y, axis=-1, keepdims=True)
            y = y * lax.rsqrt(msq + NORM_EPS) * sg_ref[...] * (1.0 - lambda_init)
        else:
            y = outs[0].T
        ov[q_sl, :] = y.astype(jnp.bfloat16)

    return scores, finish


def _attention(kind, qs, k, v, extras, lambda_init=0.0):
    b, h, s, _ = k.shape
    nblk = s // BLK
    hg = ATTN_HEADS_PER_STEP
    ways = HEADS_INTERLEAVED[kind]
    nslots = len(qs) * nblk * (nblk + 1) // 2
    h_spec = pl.BlockSpec((1, hg, s, HEAD_DIM), lambda bi, hi: (bi, hi, 0, 0))
    in_specs = [pl.BlockSpec(memory_space=pltpu.SMEM)] + [h_spec] * (len(qs) + 2)
    scratch = [pltpu.VMEM((ways, nslots, BLK, BLK), jnp.float32),
               pltpu.VMEM((ways, nslots * BLK, BLK), jnp.bfloat16),
               pltpu.VMEM((ways, HEAD_DIM, s), jnp.bfloat16)]
    if kind == MOBA:
        scratch += [pltpu.VMEM((ways, nblk, nblk, BLK), jnp.float32),
                    pltpu.VMEM((ways, GATE_ROWS + s, HEAD_DIM), jnp.bfloat16)]
    elif kind == FOX:
        in_specs.append(pl.BlockSpec((1, hg, 1, s), lambda bi, hi: (bi, hi, 0, 0)))
        scratch.append(pltpu.VMEM((ways, nblk, BLK, BLK), jnp.float32))
    else:
        in_specs += [pl.BlockSpec((4, DIFF_SUB_DIM), lambda bi, hi: (0, 0)),
                     pl.BlockSpec((1, HEAD_DIM), lambda bi, hi: (0, 0))]
    return pl.pallas_call(
        functools.partial(_attn_kernel, kind, nblk, hg, lambda_init),
        grid=(b, h // hg), in_specs=in_specs,
        out_specs=h_spec,
        out_shape=jax.ShapeDtypeStruct((b, h, s, HEAD_DIM), jnp.bfloat16),
        scratch_shapes=scratch,
        compiler_params=pltpu.CompilerParams(
            dimension_semantics=("arbitrary", "arbitrary"), vmem_limit_bytes=VMEM_LIMIT_BYTES),
        name=f"attn_{KIND_NAMES[kind]}",
    )(jnp.zeros((1,), jnp.int32), *qs, k, v, *extras)


def _gated(y_ref, sz_ref):
    y = jnp.concatenate([y_ref[0, h] for h in range(y_ref.shape[1])], axis=1)
    return (y.astype(jnp.float32) * sz_ref[0].astype(jnp.float32)).astype(jnp.bfloat16)


def _out_kernel(y_ref, sz_ref, x_ref, w_ref, o_ref):
    o_ref[0] = x_ref[0] + _dot(_gated(y_ref, sz_ref), w_ref[...])


def _out_project(y, sz, x, w, tm=1024):
    b, s, d = x.shape
    h = y.shape[1]
    width = h * HEAD_DIM
    row_spec = pl.BlockSpec((1, tm, d), lambda bi, si: (bi, si, 0))
    return pl.pallas_call(
        _out_kernel, grid=(b, s // tm),
        in_specs=[pl.BlockSpec((1, h, tm, HEAD_DIM), lambda bi, si: (bi, 0, si, 0)),
                  pl.BlockSpec((1, tm, width), lambda bi, si: (bi, si, 0)),
                  row_spec, pl.BlockSpec((width, d), lambda bi, si: (0, 0))],
        out_specs=row_spec,
        out_shape=jax.ShapeDtypeStruct((b, s, d), jnp.float32),
        compiler_params=pltpu.CompilerParams(
            dimension_semantics=("arbitrary", "arbitrary"), vmem_limit_bytes=VMEM_LIMIT_BYTES),
        name="out_proj",
    )(y, sz, x, w)


def _rope_tables(seq, dim):
    inv = 1.0 / (ROPE_THETA ** (jnp.arange(0, dim, 2, dtype=jnp.float32) / dim))
    ang = jnp.arange(seq, dtype=jnp.float32)[:, None] * inv[None, :]
    return jnp.cos(ang), jnp.sin(ang)


def kernel(x, norm_g, w_out, a_w_in, a_q_norm, a_k_norm, b_w_in, b_f_bias, b_q_norm, b_k_norm,
           c_w_in, c_q_norm, c_k_norm, c_lambda_q1, c_lambda_k1, c_lambda_q2, c_lambda_k2, c_subln):
    b, s, d = x.shape
    depth = norm_g.shape[0]
    width = N_HEADS * HEAD_DIM
    assert s % BLK == 0 and BLK == MOBA_BLOCK and d == width

    cos_h, sin_h = _rope_tables(s, HEAD_DIM)
    moba_tabs = (jnp.concatenate([cos_h, cos_h], axis=1), jnp.concatenate([-sin_h, sin_h], axis=1))
    cos_d, sin_d = _rope_tables(s, DIFF_SUB_DIM)
    zero = jnp.zeros_like(sin_d)
    diff_tabs = (jnp.concatenate([cos_d] * 4, axis=1),
                 jnp.concatenate([-sin_d, zero, -sin_d, zero], axis=1),
                 jnp.concatenate([zero, sin_d, zero, sin_d], axis=1))

    prev = None
    for i in range(depth):
        kind, j = i % 3, i // 3
        ng = norm_g[i][None, :]
        tm = 512 if prev is not None else FIRST_LAYER_TM
        if kind == MOBA:
            w = a_w_in[j].astype(jnp.bfloat16)
            outs = _project(MOBA, x, ng, w, a_q_norm[j][None, :], a_k_norm[j][None, :], moba_tabs, prev, tm)
        elif kind == FOX:
            w = jnp.pad(b_w_in[j], ((0, 0), (0, LANES - N_HEADS))).astype(jnp.bfloat16)
            fb = jnp.pad(b_f_bias[j], (0, LANES - N_HEADS))[None, :]
            outs = _project(FOX, x, ng, w, b_q_norm[j][None, :], b_k_norm[j][None, :], (fb,), prev, tm)
        else:
            w = c_w_in[j].astype(jnp.bfloat16)
            qg = jnp.concatenate([c_q_norm[j]] * 2)[None, :]
            kg = jnp.concatenate([c_k_norm[j]] * 2)[None, :]
            outs = _project(DIFF, x, ng, w, qg, kg, diff_tabs, prev, tm)
        if prev is not None:
            x, *outs = outs
        if kind == MOBA:
            q, k, v, sz = outs
            y = _attention(MOBA, (q,), k, v, ())
        elif kind == FOX:
            q, k, v, sz, c = outs
            y = _attention(FOX, (q,), k, v, (c,))
        else:
            qa, qb, k, v, sz = outs
            lam = jnp.stack([c_lambda_q1[j], c_lambda_k1[j], c_lambda_q2[j], c_lambda_k2[j]])
            lambda_init = 0.8 - 0.6 * math.exp(-0.3 * i)
            y = _attention(DIFF, (qa, qb), k, v, (lam, c_subln[j][None, :]), lambda_init)
        prev = (y, sz, w_out[i].astype(jnp.bfloat16))
    return _out_project(*prev[:2], x, prev[2])
```

```python
import functools
import math

import jax
import jax.numpy as jnp
from jax import lax
from jax.experimental import pallas as pl
from jax.experimental.pallas import tpu as pltpu

N_HEADS = 8
HEAD_DIM = 128
ROPE_THETA = 10000.0
NORM_EPS = 1e-6
MOBA_BLOCK = 256
MOBA_TOPK = 3
DIFF_SUB_DIM = HEAD_DIM // 2

BLK = 256
QK_LOOKAHEAD = (4, 4, 2)
ATTN_HEADS_PER_STEP = 4
HEADS_INTERLEAVED = (1, 2, 1)
LONG_TILES_FIRST = (False, False, False)
GATE_ROWS = 16
FIRST_LAYER_TM = 1024
LANES = 128
LOG2E = 1.4426950408889634
NEG_INF = float("-inf")
VMEM_LIMIT_BYTES = 56 * 1024 * 1024

MOBA, FOX, DIFF = 0, 1, 2
KIND_NAMES = ("moba", "fox", "diff")


def _dot(a, b):
    return jnp.dot(a, b, preferred_element_type=jnp.float32)


def _dot_nt(a, b):
    return lax.dot_general(a, b, (((1,), (1,)), ((), ())), preferred_element_type=jnp.float32)


def _head_norm_rope(kind, xh, ssq, gain, tabs, scale):
    norm_dim = DIFF_SUB_DIM if kind == DIFF else HEAD_DIM
    y = xh * lax.rsqrt(ssq * (1.0 / norm_dim) + NORM_EPS) * gain
    if kind == MOBA:
        cos_t, sin_t = tabs
        y = y * cos_t + pltpu.roll(y, HEAD_DIM // 2, axis=1) * sin_t
    elif kind == DIFF:
        cos_t, s_up, s_dn = tabs
        y = (y * cos_t + pltpu.roll(y, HEAD_DIM - DIFF_SUB_DIM // 2, axis=1) * s_up
             + pltpu.roll(y, DIFF_SUB_DIM // 2, axis=1) * s_dn)
    if scale != 1.0:
        y = y * scale
    return y


def _proj_kernel(kind, fused, tm, width, *refs):
    if fused:
        y_ref, szp_ref, wout_ref = refs[:3]
        refs = refs[3:]
    if kind == MOBA:
        (x_ref, ng_ref, w_ref, qg_ref, kg_ref, cos_ref, sin_ref, *outs, hdn_ref) = refs
        tabs = (cos_ref[...], sin_ref[...])
    elif kind == FOX:
        (x_ref, ng_ref, w_ref, qg_ref, kg_ref, fb_ref, *outs, hdn_ref, carry_ref) = refs
        tabs = None
    else:
        (x_ref, ng_ref, w_ref, qg_ref, kg_ref, cos_ref, sup_ref, sdn_ref, *outs, hdn_ref) = refs
        tabs = (cos_ref[...], sup_ref[...], sdn_ref[...])
    if fused:
        xnew_ref, *outs = outs
    if kind == MOBA:
        q_ref, k_ref, v_ref, sz_ref = outs
    elif kind == FOX:
        q_ref, k_ref, v_ref, sz_ref, c_ref = outs
    else:
        q_ref, qb_ref, k_ref, v_ref, sz_ref = outs

    if kind == FOX:
        @pl.when(pl.program_id(1) == 0)
        def _():
            carry_ref[...] = jnp.zeros_like(carry_ref)

    x = x_ref[0]
    if fused:
        x = x + _dot(_gated(y_ref, szp_ref), wout_ref[...])
        xnew_ref[0] = x
    ms = jnp.mean(x * x, axis=-1, keepdims=True)
    hdn_ref[...] = (x * lax.rsqrt(ms + NORM_EPS) * ng_ref[...]).astype(jnp.bfloat16)

    norm_dim = DIFF_SUB_DIM if kind == DIFF else HEAD_DIM
    q_scale = (norm_dim ** -0.5) * LOG2E
    pair = 2 * HEAD_DIM
    shift = norm_dim.bit_length() - 1
    gi = lax.broadcasted_iota(jnp.int32, (pair, pair), 0) >> shift
    gj = lax.broadcasted_iota(jnp.int32, (pair, pair), 1) >> shift
    group_ones = jnp.where(gi == gj, 1.0, 0.0).astype(jnp.bfloat16)

    def pair_dot(g, p):
        return _dot(hdn_ref[...], w_ref[:, g * width + p * pair:g * width + (p + 1) * pair])

    def v_store(p, rv):
        for half in range(2):
            v_ref[0, 2 * p + half] = rv[:, half * HEAD_DIM:(half + 1) * HEAD_DIM].astype(jnp.bfloat16)

    def qk_epilogue(p, rq, rk):
        if kind != FOX:
            ssq_q = _dot((rq * rq).astype(jnp.bfloat16), group_ones)
            ssq_k = _dot((rk * rk).astype(jnp.bfloat16), group_ones)
        for half in range(2):
            h = 2 * p + half
            sl = slice(half * HEAD_DIM, (half + 1) * HEAD_DIM)
            if kind == FOX:
                sq, sk = (jnp.sum(r[:, sl] * r[:, sl], axis=-1, keepdims=True) for r in (rq, rk))
            else:
                sq, sk = ssq_q[:, sl], ssq_k[:, sl]
            q = _head_norm_rope(kind, rq[:, sl], sq, qg_ref[...], tabs, q_scale)
            kk = _head_norm_rope(kind, rk[:, sl], sk, kg_ref[...], tabs, 1.0)
            if kind == DIFF:
                lane = lax.broadcasted_iota(jnp.int32, q.shape, 1)
                lo = lane < DIFF_SUB_DIM
                q_ref[0, h] = jnp.where(lo, q, 0.0).astype(jnp.bfloat16)
                qb_ref[0, h] = jnp.where(lo, 0.0, q).astype(jnp.bfloat16)
            else:
                q_ref[0, h] = q.astype(jnp.bfloat16)
            k_ref[0, h] = kk.astype(jnp.bfloat16)

    def forget_gates(fl):
        head_lanes = lax.broadcasted_iota(jnp.int32, fl.shape, 1) < N_HEADS
        logf = jnp.minimum(fl, 0.0) - jnp.log(1.0 + jnp.exp(-jnp.abs(fl)))
        logf = jnp.where(head_lanes, logf, 0.0)
        row = lax.broadcasted_iota(jnp.int32, (tm, tm), 0)
        col = lax.broadcasted_iota(jnp.int32, (tm, tm), 1)
        tril = jnp.where(row >= col, 1.0, 0.0).astype(jnp.bfloat16)
        hi = logf.astype(jnp.bfloat16).astype(jnp.float32)
        r1 = logf - hi
        mid = r1.astype(jnp.bfloat16).astype(jnp.float32)
        lo_ = (r1 - mid).astype(jnp.bfloat16).astype(jnp.float32)
        parts = hi + pltpu.roll(mid, N_HEADS, axis=1) + pltpu.roll(lo_, 2 * N_HEADS, axis=1)
        cs = _dot(tril, parts.astype(jnp.bfloat16))
        csum = cs + pltpu.roll(cs, LANES - N_HEADS, axis=1) + pltpu.roll(cs, LANES - 2 * N_HEADS, axis=1)
        csum = jnp.where(head_lanes, csum, 0.0)
        c_tile = csum + carry_ref[0:1, :]
        carry_ref[0:1, :] = c_tile[tm - 1:tm, :]
        cT = (c_tile * LOG2E).T
        for h in range(N_HEADS):
            c_ref[0, h] = cT[h:h + 1, :]

    if kind == FOX:
        fl = _dot(hdn_ref[...], w_ref[:, 4 * width:4 * width + LANES]) + fb_ref[...]
    n_pairs = N_HEADS // 2
    prev = None
    for p in range(n_pairs):
        cur = (pair_dot(0, p), pair_dot(1, p))
        v_store(p, pair_dot(2, p))
        if kind == FOX and p == 0:
            forget_gates(fl)
        if prev is not None:
            qk_epilogue(p - 1, *prev)
        prev = cur
    zc = width // 2
    rz = [None, None]
    rz[0] = _dot(hdn_ref[...], w_ref[:, 3 * width:3 * width + zc])
    qk_epilogue(n_pairs - 1, *prev)
    rz[1] = _dot(hdn_ref[...], w_ref[:, 3 * width + zc:4 * width])
    for c in range(2):
        sz_ref[0, :, c * zc:(c + 1) * zc] = (rz[c] * (1.0 / (1.0 + jnp.exp(-rz[c])))).astype(jnp.bfloat16)


def _project(kind, x, ng, w, qg, kg, extras, prev=None, tm=512):
    b, s, d = x.shape
    width = N_HEADS * HEAD_DIM
    grid = (b, s // tm)
    fused = prev is not None

    def const(shape):
        return pl.BlockSpec(shape, lambda bi, si: (0,) * len(shape))

    tab_spec = pl.BlockSpec((tm, HEAD_DIM), lambda bi, si: (si, 0))
    x_spec = pl.BlockSpec((1, tm, d), lambda bi, si: (bi, si, 0))
    h_spec = pl.BlockSpec((1, N_HEADS, tm, HEAD_DIM), lambda bi, si: (bi, 0, si, 0))
    sz_spec = pl.BlockSpec((1, tm, width), lambda bi, si: (bi, si, 0))
    in_specs = [h_spec, sz_spec, const((width, d))] if fused else []
    in_specs += [x_spec, const((1, d)), const(w.shape), const((1, HEAD_DIM)), const((1, HEAD_DIM))]
    if kind == MOBA:
        in_specs += [tab_spec, tab_spec]
    elif kind == FOX:
        in_specs += [const((1, LANES))]
    else:
        in_specs += [tab_spec, tab_spec, tab_spec]

    h_shape = jax.ShapeDtypeStruct((b, N_HEADS, s, HEAD_DIM), jnp.bfloat16)
    sz_shape = jax.ShapeDtypeStruct((b, s, width), jnp.bfloat16)

    n_heads_out = 4 if kind == DIFF else 3
    out_shape = [h_shape] * n_heads_out + [sz_shape]
    out_specs = [h_spec] * n_heads_out + [sz_spec]
    if fused:
        out_shape.insert(0, jax.ShapeDtypeStruct((b, s, d), jnp.float32))
        out_specs.insert(0, x_spec)
    scratch = [pltpu.VMEM((tm, d), jnp.bfloat16)]
    if kind == FOX:
        out_shape.append(jax.ShapeDtypeStruct((b, N_HEADS, 1, s), jnp.float32))
        out_specs.append(pl.BlockSpec((1, N_HEADS, 1, tm), lambda bi, si: (bi, 0, 0, si)))
        scratch.append(pltpu.VMEM((8, LANES), jnp.float32))

    return pl.pallas_call(
        functools.partial(_proj_kernel, kind, fused, tm, width),
        grid=grid, in_specs=in_specs, out_specs=out_specs, out_shape=out_shape,
        scratch_shapes=scratch,
        compiler_params=pltpu.CompilerParams(
            dimension_semantics=("arbitrary", "arbitrary"), vmem_limit_bytes=VMEM_LIMIT_BYTES),
        name=f"proj_{KIND_NAMES[kind]}",
    )(*(prev or ()), x, ng, w, qg, kg, *extras)


def _attn_kernel(kind, nblk, heads, lambda_init, *refs):
    zero_ref, *refs = refs
    if kind == MOBA:
        q_ref, k_ref, v_ref, o_ref, s_ref, p_ref, vT_ref, bias_ref, kx_ref = refs
        q_refs = (q_ref,)
    elif kind == FOX:
        q_ref, k_ref, v_ref, c_ref, o_ref, s_ref, p_ref, vT_ref, ck_ref = refs
        q_refs = (q_ref,)
    else:
        q_ref, qb_ref, k_ref, v_ref, lam_ref, sg_ref, o_ref, s_ref, p_ref, vT_ref = refs
        q_refs = (q_ref, qb_ref)
    ways = HEADS_INTERLEAVED[kind]
    refs = (*refs, zero_ref[0])

    def group(g, carry):
        stages = [_attn_head_stages(kind, nblk, lambda_init, refs, q_refs, w, g * ways + w) for w in range(ways)]
        pending = [{} for _ in range(ways)]
        lookahead = QK_LOOKAHEAD[kind]
        order = list(range(nblk - 1, -1, -1)) if LONG_TILES_FIRST[kind] else list(range(nblk))
        for step in range(nblk + lookahead):
            for w, (scores, finish) in enumerate(stages):
                if step < nblk:
                    pending[w][order[step]] = scores(order[step])
            done = step - lookahead
            for w, (scores, finish) in enumerate(stages):
                if done >= 0:
                    finish(order[done], pending[w].pop(order[done]))
        return carry

    lax.fori_loop(0, heads // ways, group, 0)


def _attn_head_stages(kind, nblk, lambda_init, refs, q_refs, w, hh):
    *refs, zero = refs
    if kind == MOBA:
        _, k_ref, v_ref, o_ref, s_ref, p_ref, vT_ref, bias_ref, kx_ref = refs
        bias_ref, kx_ref = bias_ref.at[w], kx_ref.at[w]
    elif kind == FOX:
        _, k_ref, v_ref, c_ref, o_ref, s_ref, p_ref, vT_ref, ck_ref = refs
        ck_ref = ck_ref.at[w]
        cv = c_ref.at[0, hh]
    else:
        _, _, k_ref, v_ref, lam_ref, sg_ref, o_ref, s_ref, p_ref, vT_ref = refs
    s_ref, p_ref, vT_ref = s_ref.at[w], p_ref.at[w], vT_ref.at[w]
    nvar = len(q_refs)
    qv = [r.at[0, hh] for r in q_refs]
    kv = k_ref.at[0, hh]
    ov = o_ref.at[0, hh]

    vT_ref[...] = v_ref[0, hh].T

    if kind == MOBA:
        assert nblk <= GATE_ROWS
        kmean = jnp.concatenate(
            [jnp.sum(kv[j * BLK:(j + 1) * BLK, :].astype(jnp.float32), axis=0, keepdims=True)
             for j in range(nblk)] + [jnp.zeros((GATE_ROWS - nblk, HEAD_DIM), jnp.float32)], axis=0)
        kx_ref[0:GATE_ROWS, :] = (kmean * (1.0 / MOBA_BLOCK)).astype(jnp.bfloat16)
        kx_ref[GATE_ROWS:GATE_ROWS + nblk * BLK, :] = kv[...]
        kv = kx_ref
        key0 = GATE_ROWS
    else:
        key0 = 0
    if kind == FOX:
        for j in range(nblk):
            ck_ref[j] = jnp.broadcast_to(cv[:, j * BLK:(j + 1) * BLK], (BLK, BLK)).T

    krow = lax.broadcasted_iota(jnp.int32, (BLK, BLK), 0)
    qcol = lax.broadcasted_iota(jnp.int32, (BLK, BLK), 1)
    causal = krow <= qcol

    def base_slot(qi, v):
        return nvar * qi * (qi + 1) // 2 + v * (qi + 1)

    def scores(qi):
        nk = qi + 1
        q_sl = slice(qi * BLK, (qi + 1) * BLK)
        k_all = kv[0:key0 + nk * BLK, :]
        shifts = []
        for v in range(nvar):
            base = base_slot(qi, v)
            s = _dot_nt(k_all, qv[v][q_sl, :])
            if kind == MOBA and qi > 0:
                row = lax.broadcasted_iota(jnp.int32, (nblk, BLK), 0)
                past = row < qi
                gm = jnp.where(past, s[0:nblk], NEG_INF)
                cnt = jnp.zeros((nblk, BLK), jnp.int32)
                for r in range(1, nblk):
                    other = pltpu.roll(gm, r, axis=0)
                    tie_first = jnp.where(row >= r, 1, 0)
                    cnt = cnt + jnp.where(other > gm, 1, jnp.where(other == gm, tie_first, 0))
                bias_ref[qi] = jnp.where(past, jnp.where(cnt < MOBA_TOPK, 0.0, NEG_INF), NEG_INF)
            m = None
            for j in range(nk):
                sj = s[key0 + j * BLK:key0 + (j + 1) * BLK]
                if kind == FOX:
                    sj = sj - ck_ref[j]
                if j == qi:
                    sj = jnp.where(causal, sj, NEG_INF)
                s_ref[base + j + zero] = sj
                cm = jnp.max(sj, axis=0, keepdims=True)
                if kind == MOBA and j < qi:
                    cm = cm + bias_ref[qi, j:j + 1, :]
                m = cm if m is None else jnp.maximum(m, cm)
            if kind == FOX:
                cq = cv[:, q_sl]
                shifts.append(cq - (m + cq))
            else:
                shifts.append(-m)
        return shifts

    def finish(qi, shifts):
        nk = qi + 1
        q_sl = slice(qi * BLK, (qi + 1) * BLK)
        v_all = vT_ref[:, 0:nk * BLK]
        outs = []
        for v in range(nvar):
            base = base_slot(qi, v)
            l = None
            for j in range(nk):
                sh = shifts[v] + bias_ref[qi, j:j + 1, :] if (kind == MOBA and j < qi) else shifts[v]
                pj = jnp.exp2(s_ref[base + j + zero] + sh)
                lj = jnp.sum(pj, axis=0, keepdims=True)
                l = lj if l is None else l + lj
                p_ref[(base + j) * BLK:(base + j + 1) * BLK, :] = pj.astype(jnp.bfloat16)
            acc = _dot(v_all, p_ref[base * BLK:(base + nk) * BLK, :])
            outs.append(acc * (1.0 / l))
        if kind == DIFF:
            lam = (jnp.exp(jnp.sum(lam_ref[0:1, :] * lam_ref[1:2, :], axis=1, keepdims=True))
                   - jnp.exp(jnp.sum(lam_ref[2:3, :] * lam_ref[3:4, :], axis=1, keepdims=True))
                   + lambda_init)
            y = (outs[0] - lam * outs[1]).T
            msq = jnp.mean(y * y, axis=-1, keepdims=True)
            y = y * lax.rsqrt(msq + NORM_EPS) * sg_ref[...] * (1.0 - lambda_init)
        else:
            y = outs[0].T
        ov[q_sl, :] = y.astype(jnp.bfloat16)

    return scores, finish


def _attention(kind, qs, k, v, extras, lambda_init=0.0):
    b, h, s, _ = k.shape
    nblk = s // BLK
    hg = ATTN_HEADS_PER_STEP
    ways = HEADS_INTERLEAVED[kind]
    nslots = len(qs) * nblk * (nblk + 1) // 2
    h_spec = pl.BlockSpec((1, hg, s, HEAD_DIM), lambda bi, hi: (bi, hi, 0, 0))
    in_specs = [pl.BlockSpec(memory_space=pltpu.SMEM)] + [h_spec] * (len(qs) + 2)
    scratch = [pltpu.VMEM((ways, nslots, BLK, BLK), jnp.float32),
               pltpu.VMEM((ways, nslots * BLK, BLK), jnp.bfloat16),
               pltpu.VMEM((ways, HEAD_DIM, s), jnp.bfloat16)]
    if kind == MOBA:
        scratch += [pltpu.VMEM((ways, nblk, nblk, BLK), jnp.float32),
                    pltpu.VMEM((ways, GATE_ROWS + s, HEAD_DIM), jnp.bfloat16)]
    elif kind == FOX:
        in_specs.append(pl.BlockSpec((1, hg, 1, s), lambda bi, hi: (bi, hi, 0, 0)))
        scratch.append(pltpu.VMEM((ways, nblk, BLK, BLK), jnp.float32))
    else:
        in_specs += [pl.BlockSpec((4, DIFF_SUB_DIM), lambda bi, hi: (0, 0)),
                     pl.BlockSpec((1, HEAD_DIM), lambda bi, hi: (0, 0))]
    return pl.pallas_call(
        functools.partial(_attn_kernel, kind, nblk, hg, lambda_init),
        grid=(b, h // hg), in_specs=in_specs,
        out_specs=h_spec,
        out_shape=jax.ShapeDtypeStruct((b, h, s, HEAD_DIM), jnp.bfloat16),
        scratch_shapes=scratch,
        compiler_params=pltpu.CompilerParams(
            dimension_semantics=("arbitrary", "arbitrary"), vmem_limit_bytes=VMEM_LIMIT_BYTES),
        name=f"attn_{KIND_NAMES[kind]}",
    )(jnp.zeros((1,), jnp.int32), *qs, k, v, *extras)


def _gated(y_ref, sz_ref):
    y = jnp.concatenate([y_ref[0, h] for h in range(y_ref.shape[1])], axis=1)
    return (y.astype(jnp.float32) * sz_ref[0].astype(jnp.float32)).astype(jnp.bfloat16)


def _out_kernel(y_ref, sz_ref, x_ref, w_ref, o_ref):
    o_ref[0] = x_ref[0] + _dot(_gated(y_ref, sz_ref), w_ref[...])


def _out_project(y, sz, x, w, tm=1024):
    b, s, d = x.shape
    h = y.shape[1]
    width = h * HEAD_DIM
    row_spec = pl.BlockSpec((1, tm, d), lambda bi, si: (bi, si, 0))
    return pl.pallas_call(
        _out_kernel, grid=(b, s // tm),
        in_specs=[pl.BlockSpec((1, h, tm, HEAD_DIM), lambda bi, si: (bi, 0, si, 0)),
                  pl.BlockSpec((1, tm, width), lambda bi, si: (bi, si, 0)),
                  row_spec, pl.BlockSpec((width, d), lambda bi, si: (0, 0))],
        out_specs=row_spec,
        out_shape=jax.ShapeDtypeStruct((b, s, d), jnp.float32),
        compiler_params=pltpu.CompilerParams(
            dimension_semantics=("arbitrary", "arbitrary"), vmem_limit_bytes=VMEM_LIMIT_BYTES),
        name="out_proj",
    )(y, sz, x, w)


def _rope_tables(seq, dim):
    inv = 1.0 / (ROPE_THETA ** (jnp.arange(0, dim, 2, dtype=jnp.float32) / dim))
    ang = jnp.arange(seq, dtype=jnp.float32)[:, None] * inv[None, :]
    return jnp.cos(ang), jnp.sin(ang)


def kernel(x, norm_g, w_out, a_w_in, a_q_norm, a_k_norm, b_w_in, b_f_bias, b_q_norm, b_k_norm,
           c_w_in, c_q_norm, c_k_norm, c_lambda_q1, c_lambda_k1, c_lambda_q2, c_lambda_k2, c_subln):
    b, s, d = x.shape
    depth = norm_g.shape[0]
    width = N_HEADS * HEAD_DIM
    assert s % BLK == 0 and BLK == MOBA_BLOCK and d == width

    cos_h, sin_h = _rope_tables(s, HEAD_DIM)
    moba_tabs = (jnp.concatenate([cos_h, cos_h], axis=1), jnp.concatenate([-sin_h, sin_h], axis=1))
    cos_d, sin_d = _rope_tables(s, DIFF_SUB_DIM)
    zero = jnp.zeros_like(sin_d)
    diff_tabs = (jnp.concatenate([cos_d] * 4, axis=1),
                 jnp.concatenate([-sin_d, zero, -sin_d, zero], axis=1),
                 jnp.concatenate([zero, sin_d, zero, sin_d], axis=1))

    prev = None
    for i in range(depth):
        kind, j = i % 3, i // 3
        ng = norm_g[i][None, :]
        tm = 512 if prev is not None else FIRST_LAYER_TM
        if kind == MOBA:
            w = a_w_in[j].astype(jnp.bfloat16)
            outs = _project(MOBA, x, ng, w, a_q_norm[j][None, :], a_k_norm[j][None, :], moba_tabs, prev, tm)
        elif kind == FOX:
            w = jnp.pad(b_w_in[j], ((0, 0), (0, LANES - N_HEADS))).astype(jnp.bfloat16)
            fb = jnp.pad(b_f_bias[j], (0, LANES - N_HEADS))[None, :]
            outs = _project(FOX, x, ng, w, b_q_norm[j][None, :], b_k_norm[j][None, :], (fb,), prev, tm)
        else:
            w = c_w_in[j].astype(jnp.bfloat16)
            qg = jnp.concatenate([c_q_norm[j]] * 2)[None, :]
            kg = jnp.concatenate([c_k_norm[j]] * 2)[None, :]
            outs = _project(DIFF, x, ng, w, qg, kg, diff_tabs, prev, tm)
        if prev is not None:
            x, *outs = outs
        if kind == MOBA:
            q, k, v, sz = outs
            y = _attention(MOBA, (q,), k, v, ())
        elif kind == FOX:
            q, k, v, sz, c = outs
            y = _attention(FOX, (q,), k, v, (c,))
        else:
            qa, qb, k, v, sz = outs
            lam = jnp.stack([c_lambda_q1[j], c_lambda_k1[j], c_lambda_q2[j], c_lambda_k2[j]])
            lambda_init = 0.8 - 0.6 * math.exp(-0.3 * i)
            y = _attention(DIFF, (qa, qb), k, v, (lam, c_subln[j][None, :]), lambda_init)
        prev = (y, sz, w_out[i].astype(jnp.bfloat16))
    return _out_project(*prev[:2], x, prev[2])
```
